```python
import math
import jax, jax.numpy as jnp
from jax import lax
import numpy as np

D_MODEL = 2048
BATCH = 1
SEQ = 8192
DEPTH = 1
DEC_BATCH = 128
DEC_SEQ = 1
PAST_LEN = 16384
PAGE_SIZE = 128

A_HEADS = 8
A_KV_HEADS = 2
A_HEAD_DIM = 128
MOBA_BLOCK = 256
MOBA_TOPK = 3
MOBA_QCHUNK = 32
B_HEADS = 8
Q_LORA = 512
KV_LORA = 512
NOPE_DIM = 128
ROPE_DIM = 64
V_DIM = 128
ROPE_THETA = 10000.0
MLA_QBLOCK = 128
REL_BUCKETS = 32
REL_MAX_DIST = 128
N_EXPERTS = 64
TOP_K = 6
D_EXPERT = 512
D_SHARED = 512
ROUTED_SCALE = 1.0
MOE_BLOCK = 128
EPS = 1e-6

A_Q_W = A_HEADS * A_HEAD_DIM
A_KV_W = A_KV_HEADS * A_HEAD_DIM
B_OUT_W = B_HEADS * V_DIM
B_QK_DIM = NOPE_DIM + ROPE_DIM
IN_SPLITS = (A_Q_W, A_KV_W, A_KV_W, Q_LORA, KV_LORA, ROPE_DIM, D_MODEL, D_MODEL)
W_IN_COLS = A_Q_W + 2 * A_KV_W + Q_LORA + KV_LORA + ROPE_DIM + 2 * D_MODEL

kernel_name = "moba_mla_gated_moe_step"

F32 = jnp.float32


def rms_norm(x, g):
    xf = x.astype(F32)
    y = xf * lax.rsqrt(jnp.mean(xf * xf, axis=-1, keepdims=True) + EPS)
    return (y * g.astype(F32)).astype(x.dtype)


def rope(x, pos):
    half = x.shape[-1] // 2
    freqs = jnp.power(ROPE_THETA, -jnp.arange(half, dtype=F32) / half)
    ang = pos.astype(F32)[:, None] * freqs
    ang = ang.reshape((ang.shape[0],) + (1,) * (x.ndim - 2) + (half,))
    cos, sin = jnp.cos(ang), jnp.sin(ang)
    xf = x.astype(F32)
    x1, x2 = xf[..., :half], xf[..., half:]
    return jnp.concatenate([x1 * cos - x2 * sin, x1 * sin + x2 * cos], axis=-1).astype(x.dtype)


def rel_bucket(dist):
    dist = jnp.maximum(dist, 0)
    max_exact = REL_BUCKETS // 2
    d = jnp.maximum(dist, 1).astype(F32)
    large = max_exact + (jnp.log(d / max_exact) / math.log(REL_MAX_DIST / max_exact)
                         * (REL_BUCKETS - max_exact)).astype(jnp.int32)
    large = jnp.minimum(large, REL_BUCKETS - 1)
    return jnp.where(dist < max_exact, dist, large)


def moba_attend(q, q_pos, k, v, rel_bias):
    n_q, L = q.shape[0], k.shape[0]
    n_blk = -(-L // MOBA_BLOCK)
    pad = n_blk * MOBA_BLOCK - L

    def blocks(t):
        t = jnp.pad(t, ((0, pad), (0, 0), (0, 0)))
        return t.reshape(n_blk, MOBA_BLOCK, A_KV_HEADS, A_HEAD_DIM).transpose(2, 0, 1, 3)

    kb, vb = blocks(k), blocks(v)
    head_kv = jnp.arange(A_HEADS) // (A_HEADS // A_KV_HEADS)
    k_mean = jnp.mean(kb.astype(F32), axis=2)[head_kv]
    blk_ids = jnp.arange(n_blk)
    offs = jnp.arange(MOBA_BLOCK)
    h_idx = jnp.arange(A_HEADS)[None, :, None, None]
    n_sel = min(MOBA_TOPK, n_blk)
    scale = A_HEAD_DIM ** -0.5
    qc = math.gcd(n_q, MOBA_QCHUNK)

    def chunk(args):
        qq, qp = args
        own = qp // MOBA_BLOCK
        gate = jnp.einsum('qhd,hnd->qhn', qq.astype(F32), k_mean)
        gate = jnp.where(blk_ids[None, None, :] < own[:, None, None], gate, -jnp.inf)
        _, top = lax.top_k(gate, n_sel)
        own_b = jnp.broadcast_to(own[:, None, None], (qc, A_HEADS, 1))
        ids = jnp.concatenate([top, own_b], axis=-1)
        blk_ok = jnp.concatenate([top < own[:, None, None], jnp.ones(own_b.shape, bool)], axis=-1)
        k_sel = kb[head_kv[None, :, None], ids]
        v_sel = vb[head_kv[None, :, None], ids]
        k_pos = ids[..., None] * MOBA_BLOCK + offs
        dist = qp[:, None, None, None] - k_pos
        ok = blk_ok[..., None] & (dist >= 0)
        logits = (jnp.einsum('qhd,qhnkd->qhnk', qq, k_sel, preferred_element_type=F32) * scale
                  + rel_bias[h_idx, rel_bucket(dist)].astype(F32))
        logits = jnp.where(ok, logits, -jnp.inf)
        p = jax.nn.softmax(logits.reshape(qc, A_HEADS, -1), axis=-1).reshape(logits.shape).astype(v.dtype)
        return jnp.einsum('qhnk,qhnkd->qhd', p, v_sel)

    out = lax.map(chunk, (q.reshape(n_q // qc, qc, A_HEADS, A_HEAD_DIM), q_pos.reshape(n_q // qc, qc)))
    return out.reshape(n_q, A_HEADS, A_HEAD_DIM)


def mla_prompt_seq(q_nope, q_pe, k_nope, k_pe, k_scale, v, pos):
    S = pos.shape[0]
    qb = math.gcd(S, MLA_QBLOCK)
    scale = B_QK_DIM ** -0.5
    ks = k_scale.T.astype(F32)[:, None, :] * scale

    def block(args):
        qn, qp, qpos = args
        s = (jnp.einsum('qhd,khd->hqk', qn, k_nope, preferred_element_type=F32)
             + jnp.einsum('qhr,kr->hqk', qp, k_pe, preferred_element_type=F32)) * ks
        s = jnp.where(pos[None, None, :] <= qpos[None, :, None], s, -jnp.inf)
        p = jax.nn.softmax(s, axis=-1).astype(v.dtype)
        return jnp.einsum('hqk,khd->qhd', p, v)

    out = lax.map(block, (q_nope.reshape(S // qb, qb, B_HEADS, NOPE_DIM),
                          q_pe.reshape(S // qb, qb, B_HEADS, ROPE_DIM),
                          pos.reshape(S // qb, qb)))
    return out.reshape(S, B_HEADS, V_DIM)


def mla_absorbed(q_nope, q_pe, q_pos, lat, k_pe, k_scale, w_uk, w_uv):
    scale = B_QK_DIM ** -0.5
    k_pos = jnp.arange(lat.shape[0])
    q_abs = jnp.einsum('qhd,chd->qhc', q_nope, w_uk)
    s = (jnp.einsum('qhc,kc->qhk', q_abs, lat, preferred_element_type=F32)
         + jnp.einsum('qhr,kr->qhk', q_pe, k_pe, preferred_element_type=F32)) * (k_scale.T.astype(F32)[None] * scale)
    s = jnp.where(k_pos[None, None, :] <= q_pos[:, None, None], s, -jnp.inf)
    p = jax.nn.softmax(s, axis=-1).astype(lat.dtype)
    o_lat = jnp.einsum('qhk,kc->qhc', p, lat)
    return jnp.einsum('qhc,chd->qhd', o_lat, w_uv)


def mixer_front(xn, pos, w_in, a_q_norm, a_k_norm, b_q_a_norm, b_q_up, b_kv_a_norm, b_w_uk, b_q_norm, b_k_norm):
    n = xn.shape[0]
    cuts = np.cumsum(IN_SPLITS)[:-1].tolist()
    aq, ak, av, bq, bkv, bkr, ga, gb = jnp.split(xn @ w_in, cuts, axis=-1)
    aq = rms_norm(aq.reshape(n, A_HEADS, A_HEAD_DIM), a_q_norm)
    ak = rms_norm(ak.reshape(n, A_KV_HEADS, A_HEAD_DIM), a_k_norm)
    av = av.reshape(n, A_KV_HEADS, A_HEAD_DIM)
    q = jnp.einsum('nc,chd->nhd', rms_norm(bq, b_q_a_norm), b_q_up)
    q = rms_norm(q, b_q_norm)
    q_nope = q[..., :NOPE_DIM] * b_k_norm[:NOPE_DIM]
    q_pe = rope(q[..., NOPE_DIM:], pos)
    lat = rms_norm(bkv, b_kv_a_norm)
    k_nope = jnp.einsum('nc,chd->nhd', lat, b_w_uk)
    kf, rf = k_nope.astype(F32), bkr.astype(F32)
    ssq = jnp.sum(kf * kf, axis=-1) + jnp.sum(rf * rf, axis=-1, keepdims=True)
    k_scale = lax.rsqrt(ssq / B_QK_DIM + EPS).astype(xn.dtype)
    k_pe = rope(bkr * b_k_norm[NOPE_DIM:], pos)
    gate_a = jax.nn.sigmoid(ga.astype(F32)).astype(xn.dtype)
    gate_b = jax.nn.sigmoid(gb.astype(F32)).astype(xn.dtype)
    return aq, ak, av, q_nope, q_pe, lat, k_nope, k_pe, k_scale, gate_a, gate_b


def branch_merge(o_a, o_b, gate_a, gate_b, w_branch_a, w_branch_b, w_out):
    return (gate_a * (o_a @ w_branch_a) + gate_b * (o_b @ w_branch_b)) @ w_out


def swiglu(x, wg, wu, wd):
    return (jax.nn.silu(x @ wg) * (x @ wu)) @ wd


def routed_experts(h, topi, topw, w_e_gate, w_e_up, w_e_down):
    n, d = h.shape
    a = n * TOP_K
    blk = int(max(8, min(MOE_BLOCK, 2 ** int(math.log2(max(1, a // N_EXPERTS))))))
    nb = -(-a // blk) + N_EXPERTS
    e_flat = topi.reshape(a)
    tok = jnp.arange(a, dtype=jnp.int32) // TOP_K
    w_flat = topw.reshape(a)
    order = jnp.argsort(e_flat)
    e_sorted = e_flat[order]
    counts = jax.ops.segment_sum(jnp.ones((a,), jnp.int32), e_flat, num_segments=N_EXPERTS)
    padded = (counts + blk - 1) // blk * blk
    pad_end = jnp.cumsum(padded)
    pad_start = pad_end - padded
    start = jnp.cumsum(counts) - counts
    dest = pad_start[e_sorted] + jnp.arange(a, dtype=jnp.int32) - start[e_sorted]
    rows = nb * blk
    row_tok = jnp.full((rows,), n, jnp.int32).at[dest].set(tok[order])
    row_w = jnp.zeros((rows,), F32).at[dest].set(w_flat[order])
    blk_expert = jnp.minimum(jnp.searchsorted(pad_end, jnp.arange(nb, dtype=jnp.int32) * blk, side='right'), N_EXPERTS - 1)
    hp = jnp.concatenate([h, jnp.zeros((1, d), h.dtype)], axis=0)
    xb = hp[row_tok].reshape(nb, blk, d)

    def run(args):
        xe, e = args
        return swiglu(xe, w_e_gate[e], w_e_up[e], w_e_down[e])

    yb = lax.map(run, (xb, blk_expert)).reshape(rows, d)
    y = jax.ops.segment_sum(yb * row_w[:, None].astype(yb.dtype), row_tok, num_segments=n + 1)
    return y[:n]


def moe(h, w_router, router_bias, w_e_gate, w_e_up, w_e_down, w_s_gate, w_s_up, w_s_down):
    scores = jax.nn.sigmoid(jnp.dot(h, w_router, preferred_element_type=F32))
    _, topi = lax.top_k(scores + router_bias.astype(F32), TOP_K)
    topw = jnp.take_along_axis(scores, topi, axis=-1)
    topw = (topw / jnp.sum(topw, axis=-1, keepdims=True) * ROUTED_SCALE).astype(h.dtype)
    return routed_experts(h, topi, topw, w_e_gate, w_e_up, w_e_down) + swiglu(h, w_s_gate, w_s_up, w_s_down)


def setup_inputs(seed: int = 0) -> dict:
    key = jax.random.key(seed)
    k = jax.random.split(key, 40)
    nrm = lambda i, shape, s: jax.random.normal(k[i], shape, F32) * s
    gain = lambda i, shape: 1.0 + 0.02 * jax.random.normal(k[i], shape, F32)
    n_pages = PAST_LEN // PAGE_SIZE
    used = DEC_BATCH * n_pages
    pool = used + max(1, used // 4)
    page_table = jax.random.permutation(k[0], pool)[:used].reshape(DEC_BATCH, n_pages).astype(jnp.int32)
    return {
        "x_prompt": nrm(1, (BATCH, SEQ, D_MODEL), 1.0),
        "x_sample": nrm(2, (DEC_BATCH, DEC_SEQ, D_MODEL), 1.0),
        "cache_moba_k": nrm(3, (DEPTH, pool, PAGE_SIZE, A_KV_HEADS, A_HEAD_DIM), 1.0),
        "cache_moba_v": nrm(4, (DEPTH, pool, PAGE_SIZE, A_KV_HEADS, A_HEAD_DIM), 1.0),
        "cache_mla_latent": nrm(5, (DEPTH, pool, PAGE_SIZE, KV_LORA), 1.0),
        "cache_mla_krope": nrm(6, (DEPTH, pool, PAGE_SIZE, ROPE_DIM), 1.0),
        "cache_mla_kscale": jax.random.uniform(k[7], (DEPTH, pool, PAGE_SIZE, B_HEADS), F32, 0.8, 1.2),
        "page_table": page_table,
        "rel_bias": nrm(8, (A_HEADS, REL_BUCKETS), 0.5),
        "attn_norm": gain(9, (DEPTH, D_MODEL)),
        "w_in": nrm(10, (DEPTH, D_MODEL, W_IN_COLS), D_MODEL ** -0.5),
        "a_q_norm": gain(11, (DEPTH, A_HEAD_DIM)),
        "a_k_norm": gain(12, (DEPTH, A_HEAD_DIM)),
        "b_q_a_norm": gain(13, (DEPTH, Q_LORA)),
        "b_q_up": nrm(14, (DEPTH, Q_LORA, B_HEADS, B_QK_DIM), Q_LORA ** -0.5),
        "b_kv_a_norm": gain(15, (DEPTH, KV_LORA)),
        "b_w_uk": nrm(16, (DEPTH, KV_LORA, B_HEADS, NOPE_DIM), KV_LORA ** -0.5),
        "b_w_uv": nrm(17, (DEPTH, KV_LORA, B_HEADS, V_DIM), KV_LORA ** -0.5),
        "b_q_norm": gain(18, (DEPTH, B_QK_DIM)),
        "b_k_norm": gain(19, (DEPTH, B_QK_DIM)),
        "w_branch_a": nrm(20, (DEPTH, A_Q_W, D_MODEL), A_Q_W ** -0.5),
        "w_branch_b": nrm(21, (DEPTH, B_OUT_W, D_MODEL), B_OUT_W ** -0.5),
        "w_out": nrm(22, (DEPTH, D_MODEL, D_MODEL), D_MODEL ** -0.5),
        "ffn_norm": gain(23, (DEPTH, D_MODEL)),
        "w_router": nrm(24, (DEPTH, D_MODEL, N_EXPERTS), D_MODEL ** -0.5),
        "router_bias": nrm(25, (DEPTH, N_EXPERTS), 0.01),
        "w_e_gate": nrm(26, (DEPTH, N_EXPERTS, D_MODEL, D_EXPERT), D_MODEL ** -0.5),
        "w_e_up": nrm(27, (DEPTH, N_EXPERTS, D_MODEL, D_EXPERT), D_MODEL ** -0.5),
        "w_e_down": nrm(28, (DEPTH, N_EXPERTS, D_EXPERT, D_MODEL), D_EXPERT ** -0.5),
        "w_s_gate": nrm(29, (DEPTH, D_MODEL, D_SHARED), D_MODEL ** -0.5),
        "w_s_up": nrm(30, (DEPTH, D_MODEL, D_SHARED), D_MODEL ** -0.5),
        "w_s_down": nrm(31, (DEPTH, D_SHARED, D_MODEL), D_SHARED ** -0.5),
    }


def reference(x_prompt, x_sample, cache_moba_k, cache_moba_v, cache_mla_latent, cache_mla_krope, cache_mla_kscale,
              page_table, rel_bias, attn_norm, w_in, a_q_norm, a_k_norm, b_q_a_norm, b_q_up, b_kv_a_norm, b_w_uk,
              b_w_uv, b_q_norm, b_k_norm, w_branch_a, w_branch_b, w_out, ffn_norm, w_router, router_bias,
              w_e_gate, w_e_up, w_e_down, w_s_gate, w_s_up, w_s_down):
    b_p, s_p, _ = x_prompt.shape
    b_d, s_d, _ = x_sample.shape
    n_p, n_d = b_p * s_p, b_d * s_d
    past = page_table.shape[1] * cache_moba_k.shape[2]
    pos_p = jnp.arange(s_p, dtype=jnp.int32)
    pos_d = past + jnp.arange(s_d, dtype=jnp.int32)
    seq_p = lambda t: t.reshape((b_p, s_p) + t.shape[1:])
    seq_d = lambda t: t.reshape((b_d, s_d) + t.shape[1:])
    hp, hd = x_prompt, x_sample
    new_p = [[], [], [], [], []]
    new_d = [[], [], [], [], []]
    for l in range(DEPTH):
        front_w = (w_in[l], a_q_norm[l], a_k_norm[l], b_q_a_norm[l], b_q_up[l], b_kv_a_norm[l], b_w_uk[l],
                   b_q_norm[l], b_k_norm[l])
        merge_w = (w_branch_a[l], w_branch_b[l], w_out[l])
        moe_w = (w_router[l], router_bias[l], w_e_gate[l], w_e_up[l], w_e_down[l], w_s_gate[l], w_s_up[l], w_s_down[l])

        aq, ak, av, qn, qpe, lat, knope, kpe, ksc, ga, gb = mixer_front(
            rms_norm(hp, attn_norm[l]).reshape(n_p, D_MODEL), jnp.tile(pos_p, b_p), *front_w)
        o_a = jax.vmap(lambda q, kk, vv: moba_attend(q, pos_p, kk, vv, rel_bias))(seq_p(aq), seq_p(ak), seq_p(av))
        v_b = jnp.einsum('nc,chd->nhd', lat, b_w_uv[l])
        o_b = jax.vmap(lambda a1, a2, a3, a4, a5, a6: mla_prompt_seq(a1, a2, a3, a4, a5, a6, pos_p))(
            seq_p(qn), seq_p(qpe), seq_p(knope), seq_p(kpe), seq_p(ksc), seq_p(v_b))
        hp = hp + branch_merge(o_a.reshape(n_p, A_Q_W), o_b.reshape(n_p, B_OUT_W), ga, gb, *merge_w).reshape(b_p, s_p, D_MODEL)
        hp = hp + moe(rms_norm(hp, ffn_norm[l]).reshape(n_p, D_MODEL), *moe_w).reshape(b_p, s_p, D_MODEL)
        for lst, t in zip(new_p, (ak, av, lat, kpe, ksc)):
            lst.append(seq_p(t))

        aq, ak, av, qn, qpe, lat, knope, kpe, ksc, ga, gb = mixer_front(
            rms_norm(hd, attn_norm[l]).reshape(n_d, D_MODEL), jnp.tile(pos_d, b_d), *front_w)

        def one_seq(args, l=l):
            pt, q_a, k_a, v_a, q_n, q_p, lat_n, kpe_n, ksc_n = args
            rows = lambda c, new: jnp.concatenate([c[l, pt].reshape((past,) + c.shape[3:]), new], axis=0)
            o_a_s = moba_attend(q_a, pos_d, rows(cache_moba_k, k_a), rows(cache_moba_v, v_a), rel_bias)
            o_b_s = mla_absorbed(q_n, q_p, pos_d, rows(cache_mla_latent, lat_n), rows(cache_mla_krope, kpe_n),
                                 rows(cache_mla_kscale, ksc_n), b_w_uk[l], b_w_uv[l])
            return o_a_s, o_b_s

        o_a, o_b = lax.map(one_seq, (page_table, seq_d(aq), seq_d(ak), seq_d(av), seq_d(qn), seq_d(qpe),
                                     seq_d(lat), seq_d(kpe), seq_d(ksc)))
        hd = hd + branch_merge(o_a.reshape(n_d, A_Q_W), o_b.reshape(n_d, B_OUT_W), ga, gb, *merge_w).reshape(b_d, s_d, D_MODEL)
        hd = hd + moe(rms_norm(hd, ffn_norm[l]).reshape(n_d, D_MODEL), *moe_w).reshape(b_d, s_d, D_MODEL)
        for lst, t in zip(new_d, (ak, av, lat, kpe, ksc)):
            lst.append(seq_d(t))

    y_prompt, y_sample = hp, hd
    moba_k_p, moba_v_p, lat_p, krope_p, kscale_p = [jnp.stack(t, axis=0) for t in new_p]
    moba_k_d, moba_v_d, lat_d, krope_d, kscale_d = [jnp.stack(t, axis=0) for t in new_d]
    return (y_prompt, y_sample, moba_k_p, moba_v_p, lat_p, krope_p, kscale_p,
            moba_k_d, moba_v_d, lat_d, krope_d, kscale_d)
```

```python
import functools
import math

import numpy as np
import jax
import jax.numpy as jnp
from jax import lax
from jax.experimental import pallas as pl
from jax.experimental.pallas import tpu as pltpu

F32 = jnp.float32
BF16 = jnp.bfloat16

A_HEADS = 8
A_KV_HEADS = 2
A_HEAD_DIM = 128
A_GROUP = A_HEADS // A_KV_HEADS
MOBA_BLOCK = 256
MOBA_TOPK = 3
B_HEADS = 8
Q_LORA = 512
KV_LORA = 512
NOPE_DIM = 128
ROPE_DIM = 64
V_DIM = 128
B_QK_DIM = NOPE_DIM + ROPE_DIM
ROPE_THETA = 10000.0
REL_BUCKETS = 32
REL_MAX_DIST = 128
N_EXPERTS = 64
TOP_K = 6
ROUTED_SCALE = 1.0
EPS = 1e-6

LANES = 128
MLA_HEAD_PAD = 256
NEG = -1e30
VMEM_LIMIT = 56 * 1024 * 1024

A_Q_W = A_HEADS * A_HEAD_DIM
A_KV_W = A_KV_HEADS * A_HEAD_DIM

COL_GA = 0
COL_GB = 2048
COL_AQ = 4096
COL_AK = 5120
COL_AV = 5376
COL_BQ = 5632
COL_BKV = 6144
COL_BKR = 6656
W_CAT_COLS = 6912


def _cparams(sem):
    return pltpu.CompilerParams(dimension_semantics=sem, vmem_limit_bytes=VMEM_LIMIT)


def _dot(a, b):
    return jnp.dot(a, b, preferred_element_type=F32)


def _dot_nt(a, b):
    return lax.dot_general(a, b, (((1,), (1,)), ((), ())), preferred_element_type=F32)


def _split_bf16(a):
    hi = a.astype(BF16)
    lo = (a - hi.astype(F32)).astype(BF16)
    return hi, lo


def _dot3(a, b):
    ah, al = _split_bf16(a)
    bh, bl = _split_bf16(b)
    return _dot(ah, bh) + (_dot(ah, bl) + _dot(al, bh))


def _lane_iota(shape):
    return lax.broadcasted_iota(jnp.int32, shape, len(shape) - 1)


def _rmsnorm_kernel(x_ref, g_ref, o_ref):
    x = x_ref[...]
    inv = lax.rsqrt(jnp.mean(x * x, axis=-1, keepdims=True) + EPS)
    o_ref[...] = (x * inv * g_ref[...]).astype(o_ref.dtype)


def _rmsnorm_bf16(x, g, tm):
    n, d = x.shape
    return pl.pallas_call(
        _rmsnorm_kernel,
        grid=(n // tm,),
        in_specs=[pl.BlockSpec((tm, d), lambda i: (i, 0)), pl.BlockSpec((1, d), lambda i: (0, 0))],
        out_specs=pl.BlockSpec((tm, d), lambda i: (i, 0)),
        out_shape=jax.ShapeDtypeStruct((n, d), BF16),
        compiler_params=_cparams(("parallel",)),
        name="rmsnorm",
    )(x, g)


def _matmul_kernel(a_ref, b_ref, o_ref):
    o_ref[...] = _dot(a_ref[...], b_ref[...])


def _matmul(a, b, tm, tn):
    m, k = a.shape
    _, n = b.shape
    return pl.pallas_call(
        _matmul_kernel,
        grid=(n // tn, m // tm),
        in_specs=[pl.BlockSpec((tm, k), lambda j, i: (i, 0)), pl.BlockSpec((k, tn), lambda j, i: (0, j))],
        out_specs=pl.BlockSpec((tm, tn), lambda j, i: (i, j)),
        out_shape=jax.ShapeDtypeStruct((m, n), F32),
        compiler_params=_cparams(("parallel", "parallel")),
        name="in_proj",
    )(a, b)


def _front_kernel(aq_ref, ak_ref, av_ref, bq_ref, bkv_ref, bkr_ref, cs_ref,
                  gaq_ref, gak_ref, gbq_ref, gbkv_ref, gq_ref, gk_ref,
                  wq_ref, wuk_ref, wuv_ref,
                  qa_ref, ka_ref, va_ref, kaug_ref, vbf_ref, kmean_ref,
                  qm_ref, km_ref, vm_ref, lat_ref, kpe_ref, ksc_ref, *, tm):
    pid = pl.program_id(0)
    lane = _lane_iota((tm, LANES))
    lo = lane < ROPE_DIM
    cs = cs_ref[...]

    def head_norm(x, g):
        inv = lax.rsqrt(jnp.mean(x * x, axis=-1, keepdims=True) + EPS)
        return x * inv * g

    for h in range(A_HEADS):
        sl = slice(h * A_HEAD_DIM, (h + 1) * A_HEAD_DIM)
        qa_ref[:, sl] = head_norm(aq_ref[:, sl], gaq_ref[:, sl])
    row = lax.broadcasted_iota(jnp.int32, (tm, LANES), 0) + pid * tm
    onehot = jnp.where(lane == row // MOBA_BLOCK, 1.0, 0.0).astype(BF16)
    for g in range(A_KV_HEADS):
        sl = slice(g * A_HEAD_DIM, (g + 1) * A_HEAD_DIM)
        k = head_norm(ak_ref[:, sl], gak_ref[:, sl])
        ka_ref[:, sl] = k
        kmean_ref[0, :, sl] = jnp.mean(k, axis=0, keepdims=True)
        kaug_ref[:, 2 * g * LANES:(2 * g + 1) * LANES] = k.astype(BF16)
        kaug_ref[:, (2 * g + 1) * LANES:(2 * g + 2) * LANES] = onehot
    av = av_ref[...]
    va_ref[...] = av
    vbf_ref[...] = av.astype(BF16)

    qc = head_norm(bq_ref[...], gbq_ref[...]).astype(BF16)
    qraw = _dot(qc, wq_ref[...])
    for h in range(B_HEADS):
        a = qraw[:, h * MLA_HEAD_PAD:h * MLA_HEAD_PAD + LANES]
        b = qraw[:, h * MLA_HEAD_PAD + LANES:(h + 1) * MLA_HEAD_PAD]
        ssq = jnp.sum(a * a, axis=-1, keepdims=True) + jnp.sum(jnp.where(lo, b * b, 0.0), axis=-1, keepdims=True)
        inv = lax.rsqrt(ssq / B_QK_DIM + EPS)
        qm_ref[:, h * MLA_HEAD_PAD:h * MLA_HEAD_PAD + LANES] = (
            a * inv * gq_ref[:, h * MLA_HEAD_PAD:h * MLA_HEAD_PAD + LANES]).astype(BF16)
        t = b * inv * gq_ref[:, h * MLA_HEAD_PAD + LANES:(h + 1) * MLA_HEAD_PAD] * cs
        pe = jnp.where(lo, t + pltpu.roll(t, ROPE_DIM, axis=1), 0.0)
        qm_ref[:, h * MLA_HEAD_PAD + LANES:(h + 1) * MLA_HEAD_PAD] = pe.astype(BF16)

    lat = head_norm(bkv_ref[...], gbkv_ref[...])
    lat_ref[...] = lat
    latb = lat.astype(BF16)
    knope = _dot(latb, wuk_ref[...])
    vm_ref[...] = _dot(latb, wuv_ref[...]).astype(BF16)
    bkr = bkr_ref[...]
    rss = jnp.sum(jnp.where(lo, bkr * bkr, 0.0), axis=-1, keepdims=True)
    t = bkr * gk_ref[...] * cs
    kpe = jnp.where(lo, t + pltpu.roll(t, ROPE_DIM, axis=1), 0.0)
    kpe_ref[...] = kpe[:, :ROPE_DIM]
    ksw = jnp.zeros((tm, LANES), F32)
    scale = B_QK_DIM ** -0.5
    for h in range(B_HEADS):
        kh = knope[:, h * NOPE_DIM:(h + 1) * NOPE_DIM]
        ssq = jnp.sum(kh * kh, axis=-1, keepdims=True) + rss
        ks = lax.rsqrt(ssq / B_QK_DIM + EPS)
        ksw = jnp.where(lane == h, ks, ksw)
        km_ref[:, h * MLA_HEAD_PAD:h * MLA_HEAD_PAD + LANES] = (kh * (ks * scale)).astype(BF16)
        km_ref[:, h * MLA_HEAD_PAD + LANES:(h + 1) * MLA_HEAD_PAD] = (kpe * (ks * scale)).astype(BF16)
    ksc_ref[...] = ksw[:, :B_HEADS]


def _front(proj, cs, prm, tm):
    n = proj.shape[0]
    nt = n // tm
    row = lambda w, c: pl.BlockSpec((tm, w), lambda i, c=c, w=w: (i, c // w))
    const = lambda a: pl.BlockSpec(a.shape, lambda i: (0,) * a.ndim)
    consts = [prm["gaq"], prm["gak"], prm["gbq"], prm["gbkv"], prm["gq"], prm["gk"], prm["wq"], prm["wuk"], prm["wuv"]]
    out_shapes = [
        ((n, A_Q_W), F32), ((n, A_KV_W), F32), ((n, A_KV_W), F32),
        ((n, 2 * A_KV_W), BF16), ((n, A_KV_W), BF16),
        ((nt, 1, A_KV_W), F32),
        ((n, B_HEADS * MLA_HEAD_PAD), BF16), ((n, B_HEADS * MLA_HEAD_PAD), BF16), ((n, B_HEADS * V_DIM), BF16),
        ((n, KV_LORA), F32), ((n, ROPE_DIM), F32), ((n, B_HEADS), F32),
    ]
    out_specs = []
    for shp, _ in out_shapes:
        if len(shp) == 3:
            out_specs.append(pl.BlockSpec((1, 1, shp[2]), lambda i: (i, 0, 0)))
        else:
            out_specs.append(pl.BlockSpec((tm, shp[1]), lambda i: (i, 0)))
    return pl.pallas_call(
        functools.partial(_front_kernel, tm=tm),
        grid=(nt,),
        in_specs=[row(A_Q_W, COL_AQ), row(A_KV_W, COL_AK), row(A_KV_W, COL_AV), row(Q_LORA, COL_BQ),
                  row(KV_LORA, COL_BKV), row(LANES, COL_BKR), pl.BlockSpec((tm, LANES), lambda i: (i, 0))]
                 + [const(a) for a in consts],
        out_specs=out_specs,
        out_shape=[jax.ShapeDtypeStruct(s, d) for s, d in out_shapes],
        compiler_params=_cparams(("parallel",)),
        name="front",
    )(proj, proj, proj, proj, proj, proj, cs, *consts)


def _top_blocks(gate, valid, lane):
    g = jnp.where(valid, gate, NEG)
    selv = jnp.zeros(gate.shape, F32)
    for _ in range(MOBA_TOPK):
        mx = jnp.max(g, axis=-1, keepdims=True)
        idx = jnp.min(jnp.where(g == mx, lane, LANES), axis=-1, keepdims=True)
        pick = lane == idx
        selv = jnp.where(pick, 1.0, selv)
        g = jnp.where(pick, -3e38, g)
    return jnp.where(valid, selv, 0.0)


def _moba_prompt_kernel(q_ref, kaug_ref, v_ref, kmean_ref, tdiag_ref, tprev_ref, o_ref,
                        qaug_ref, m_ref, l_ref, acc_ref):
    i = pl.program_id(1)
    blk = MOBA_BLOCK
    lane = _lane_iota((blk, LANES))
    valid = lane < i
    kmean = kmean_ref[...]
    scale = A_HEAD_DIM ** -0.5

    diag0 = pl.multiple_of(i * blk, blk)
    kd = kaug_ref[pl.ds(diag0, blk), :]
    vd = v_ref[pl.ds(diag0, blk), :]
    for hh in range(A_GROUP):
        q = q_ref[:, hh * A_HEAD_DIM:(hh + 1) * A_HEAD_DIM]
        qh, ql = _split_bf16(q)
        kh, kl = _split_bf16(kmean)
        gate = _dot_nt(qh, kh) + (_dot_nt(qh, kl) + _dot_nt(ql, kh))
        selv = _top_blocks(gate, valid, lane)
        on = jnp.where(lane == i, 1.0, selv)
        qaug_ref[hh, :, :LANES] = (q * scale).astype(BF16)
        qaug_ref[hh, :, LANES:] = jnp.where(on > 0.0, 0.0, NEG).astype(BF16)
        s = _dot_nt(qaug_ref[hh], kd) + tdiag_ref[hh]
        m = jnp.max(s, axis=-1, keepdims=True)
        p = jnp.exp(s - m)
        m_ref[hh] = m
        l_ref[hh] = jnp.sum(p, axis=-1, keepdims=True)
        acc_ref[hh] = _dot(p.astype(BF16), vd)

    def step(j0, bias_ref):
        kj = kaug_ref[pl.ds(j0, blk), :]
        vj = v_ref[pl.ds(j0, blk), :]
        for hh in range(A_GROUP):
            s = _dot_nt(qaug_ref[hh], kj)
            if bias_ref is not None:
                s = s + bias_ref[hh]
            m_old = m_ref[hh]
            m_new = jnp.maximum(m_old, jnp.max(s, axis=-1, keepdims=True))
            alpha = jnp.exp(m_old - m_new)
            p = jnp.exp(s - m_new)
            l_ref[hh] = alpha * l_ref[hh] + jnp.sum(p, axis=-1, keepdims=True)
            acc_ref[hh] = alpha * acc_ref[hh] + _dot(p.astype(BF16), vj)
            m_ref[hh] = m_new

    @pl.when(i >= 1)
    def _():
        step(pl.multiple_of((i - 1) * blk, blk), tprev_ref)

    def far(j, carry):
        step(pl.multiple_of(j * blk, blk), None)
        return carry

    lax.fori_loop(0, jnp.maximum(i - 1, 0), far, 0)

    for hh in range(A_GROUP):
        o_ref[:, hh * A_HEAD_DIM:(hh + 1) * A_HEAD_DIM] = (acc_ref[hh] / l_ref[hh]).astype(o_ref.dtype)


def _moba_prompt(qa, kaug, vbf, kmean_pad, tdiag, tprev):
    s = qa.shape[0]
    nblk = s // MOBA_BLOCK
    gw = A_GROUP * A_HEAD_DIM
    return pl.pallas_call(
        _moba_prompt_kernel,
        grid=(A_KV_HEADS, nblk),
        in_specs=[
            pl.BlockSpec((MOBA_BLOCK, gw), lambda g, i: (i, g)),
            pl.BlockSpec((s, 2 * LANES), lambda g, i: (0, g)),
            pl.BlockSpec((s, A_HEAD_DIM), lambda g, i: (0, g)),
            pl.BlockSpec((None, LANES, A_HEAD_DIM), lambda g, i: (g, 0, 0)),
            pl.BlockSpec((A_GROUP, MOBA_BLOCK, MOBA_BLOCK), lambda g, i: (g, 0, 0)),
            pl.BlockSpec((A_GROUP, MOBA_BLOCK, MOBA_BLOCK), lambda g, i: (g, 0, 0)),
        ],
        out_specs=pl.BlockSpec((MOBA_BLOCK, gw), lambda g, i: (i, g)),
        out_shape=jax.ShapeDtypeStruct((s, A_Q_W), BF16),
        scratch_shapes=[
            pltpu.VMEM((A_GROUP, MOBA_BLOCK, 2 * LANES), BF16),
            pltpu.VMEM((A_GROUP, MOBA_BLOCK, 1), F32),
            pltpu.VMEM((A_GROUP, MOBA_BLOCK, 1), F32),
            pltpu.VMEM((A_GROUP, MOBA_BLOCK, A_HEAD_DIM), F32),
        ],
        compiler_params=_cparams(("parallel", "parallel")),
        name="moba_prompt",
    )(qa, kaug, vbf, kmean_pad, tdiag, tprev)


def _mla_prompt_kernel(q_ref, k_ref, v_ref, o_ref, m_ref, l_ref, acc_ref, *, tq):
    qi = pl.program_id(1)
    q = q_ref[...]
    d0 = pl.multiple_of(qi * tq, tq)
    s = _dot_nt(q, k_ref[pl.ds(d0, tq), :])
    r = lax.broadcasted_iota(jnp.int32, (tq, tq), 0)
    c = lax.broadcasted_iota(jnp.int32, (tq, tq), 1)
    s = jnp.where(c <= r, s, NEG)
    m = jnp.max(s, axis=-1, keepdims=True)
    p = jnp.exp(s - m)
    m_ref[...] = m
    l_ref[...] = jnp.sum(p, axis=-1, keepdims=True)
    acc_ref[...] = _dot(p.astype(BF16), v_ref[pl.ds(d0, tq), :])

    def body(j, carry):
        j0 = pl.multiple_of(j * tq, tq)
        s = _dot_nt(q, k_ref[pl.ds(j0, tq), :])
        m_old = m_ref[...]
        m_new = jnp.maximum(m_old, jnp.max(s, axis=-1, keepdims=True))
        alpha = jnp.exp(m_old - m_new)
        p = jnp.exp(s - m_new)
        l_ref[...] = alpha * l_ref[...] + jnp.sum(p, axis=-1, keepdims=True)
        acc_ref[...] = alpha * acc_ref[...] + _dot(p.astype(BF16), v_ref[pl.ds(j0, tq), :])
        m_ref[...] = m_new
        return carry

    lax.fori_loop(0, qi, body, 0)
    o_ref[...] = (acc_ref[...] / l_ref[...]).astype(o_ref.dtype)


def _mla_prompt(qm, km, vm, tq):
    s = qm.shape[0]
    return pl.pallas_call(
        functools.partial(_mla_prompt_kernel, tq=tq),
        grid=(B_HEADS, s // tq),
        in_specs=[
            pl.BlockSpec((tq, MLA_HEAD_PAD), lambda h, i: (i, h)),
            pl.BlockSpec((s, MLA_HEAD_PAD), lambda h, i: (0, h)),
            pl.BlockSpec((s, V_DIM), lambda h, i: (0, h)),
        ],
        out_specs=pl.BlockSpec((tq, V_DIM), lambda h, i: (i, h)),
        out_shape=jax.ShapeDtypeStruct((s, B_HEADS * V_DIM), BF16),
        scratch_shapes=[pltpu.VMEM((tq, 1), F32), pltpu.VMEM((tq, 1), F32), pltpu.VMEM((tq, V_DIM), F32)],
        compiler_params=_cparams(("parallel", "parallel")),
        name="mla_prompt",
    )(qm, km, vm)


def _moba_decode_kernel(pt_ref, *refs, pages_per_step, n_blocks):
    del pt_ref
    pps = pages_per_step
    k_refs = refs[:pps]
    v_refs = refs[pps:2 * pps]
    q_ref, knew_ref, vnew_ref, tlast_ref, bias0_ref = refs[2 * pps:2 * pps + 5]
    o_ref = refs[2 * pps + 5]
    m_all, l_all, g_all, o_all = refs[2 * pps + 6:]
    c = pl.program_id(1)
    nc = pl.num_programs(1)
    page = k_refs[0].shape[0] // A_KV_HEADS
    ppb = MOBA_BLOCK // page
    bps = pps // ppb
    lane = _lane_iota((A_HEADS, LANES))
    head = lax.broadcasted_iota(jnp.int32, (A_HEADS, LANES), 0)
    first_group = head < A_GROUP
    q = q_ref[...]
    qs = (q * (A_HEAD_DIM ** -0.5)).astype(BF16)

    @pl.when(c == 0)
    def _():
        m_all[...] = jnp.full((A_HEADS, LANES), NEG, F32)
        l_all[...] = jnp.zeros((A_HEADS, LANES), F32)
        g_all[...] = jnp.full((A_HEADS, LANES), NEG, F32)

    for nb in range(bps):
        jg = c * bps + nb
        s_parts, ksum = [], [jnp.zeros((1, A_HEAD_DIM), F32) for _ in range(A_KV_HEADS)]
        vbs = []
        for pp in range(ppb):
            kr = k_refs[nb * ppb + pp]
            vr = v_refs[nb * ppb + pp]
            s_g, v_g = [], []
            for g in range(A_KV_HEADS):
                kp = kr[pl.ds(g, page, stride=A_KV_HEADS), :]
                ksum[g] = ksum[g] + jnp.sum(kp, axis=0, keepdims=True)
                s_g.append(_dot_nt(qs, kp.astype(BF16)))
                v_g.append(vr[pl.ds(g, page, stride=A_KV_HEADS), :].astype(BF16))
            s_parts.append(jnp.where(first_group, s_g[0], s_g[1]))
            vbs.append(v_g)
        s = jnp.concatenate(s_parts, axis=1)
        s = s + jnp.where(jg == n_blocks - 1, tlast_ref[...], 0.0)
        m = jnp.max(s, axis=-1, keepdims=True)
        p = jnp.exp(s - m)
        l = jnp.sum(p, axis=-1, keepdims=True)
        pb = p.astype(BF16)
        o_g = []
        for g in range(A_KV_HEADS):
            vb = jnp.concatenate([vbs[pp][g] for pp in range(ppb)], axis=0)
            o_g.append(_dot(pb, vb))
        o_all[jg] = jnp.where(first_group, o_g[0], o_g[1])
        kmean = jnp.where(first_group, ksum[0], ksum[1]) * (1.0 / MOBA_BLOCK)
        gate = jnp.sum(q * kmean, axis=-1, keepdims=True)
        hit = lane == jg
        m_all[...] = jnp.where(hit, m, m_all[...])
        l_all[...] = jnp.where(hit, l, l_all[...])
        g_all[...] = jnp.where(hit, gate, g_all[...])

    @pl.when(c == nc - 1)
    def _():
        selv = _top_blocks(g_all[...], lane < n_blocks, lane)
        s_new = jnp.sum(qs.astype(F32) * knew_ref[...].astype(BF16).astype(F32), axis=-1, keepdims=True) + bias0_ref[...]
        ms = jnp.where(selv > 0.0, m_all[...], NEG)
        mx = jnp.maximum(jnp.max(ms, axis=-1, keepdims=True), s_new)
        w = jnp.where(selv > 0.0, jnp.exp(ms - mx), 0.0)
        p_new = jnp.exp(s_new - mx)
        den = jnp.sum(w * l_all[...], axis=-1, keepdims=True) + p_new
        acc = p_new * vnew_ref[...]
        for j in range(n_blocks):
            acc = acc + w[:, j:j + 1] * o_all[j]
        o_ref[...] = acc / den


def _moba_decode(page_table, ck, cv, q, knew, vnew, tlast, bias0, pages_per_step=16):
    nseq, n_pages = page_table.shape
    pool, rows, _ = ck.shape
    page = rows // A_KV_HEADS
    n_blocks = n_pages * page // MOBA_BLOCK
    pps = pages_per_step
    assert n_pages % pps == 0 and n_blocks <= LANES

    def page_spec(p):
        return pl.BlockSpec((None, rows, A_HEAD_DIM), lambda b, c, pt, p=p: (pt[b, c * pps + p], 0, 0))

    per_seq = lambda w: pl.BlockSpec((None, A_HEADS, w), lambda b, c, pt: (b, 0, 0))
    grid_spec = pltpu.PrefetchScalarGridSpec(
        num_scalar_prefetch=1,
        grid=(nseq, n_pages // pps),
        in_specs=[page_spec(p) for p in range(pps)] + [page_spec(p) for p in range(pps)]
                 + [per_seq(A_HEAD_DIM), per_seq(A_HEAD_DIM), per_seq(A_HEAD_DIM),
                    pl.BlockSpec((A_HEADS, MOBA_BLOCK), lambda b, c, pt: (0, 0)),
                    pl.BlockSpec((A_HEADS, 1), lambda b, c, pt: (0, 0))],
        out_specs=per_seq(A_HEAD_DIM),
        scratch_shapes=[pltpu.VMEM((A_HEADS, LANES), F32), pltpu.VMEM((A_HEADS, LANES), F32),
                        pltpu.VMEM((A_HEADS, LANES), F32), pltpu.VMEM((n_blocks, A_HEADS, A_HEAD_DIM), F32)],
    )
    return pl.pallas_call(
        functools.partial(_moba_decode_kernel, pages_per_step=pps, n_blocks=n_blocks),
        grid_spec=grid_spec,
        out_shape=jax.ShapeDtypeStruct((nseq, A_HEADS, A_HEAD_DIM), F32),
        compiler_params=_cparams(("parallel", "arbitrary")),
        name="moba_decode",
    )(page_table, *([ck] * pps), *([cv] * pps), q, knew, vnew, tlast, bias0)


def _mla_decode_kernel(pt_ref, *refs, pages_per_step):
    del pt_ref
    pps = pages_per_step
    lat_refs = refs[:pps]
    kr_refs = refs[pps:2 * pps]
    ks_refs = refs[2 * pps:3 * pps]
    qabs_t_ref, qpe_t_ref, eye_ref, qabs_ref, qpe_ref, latn_ref, kpen_ref, kscn_ref = refs[3 * pps:3 * pps + 8]
    o_ref = refs[3 * pps + 8]
    m_ref, l_ref, acc_ref = refs[3 * pps + 9:]
    c = pl.program_id(1)
    nc = pl.num_programs(1)
    scale = B_QK_DIM ** -0.5

    @pl.when(c == 0)
    def _():
        m_ref[...] = jnp.full(m_ref.shape, NEG, F32)
        l_ref[...] = jnp.zeros(l_ref.shape, F32)
        acc_ref[...] = jnp.zeros(acc_ref.shape, F32)

    qabs_t = qabs_t_ref[...]
    qpe_t = qpe_t_ref[...]
    eye = eye_ref[...]
    s_parts, lat_pairs = [], []
    for n in range(pps // 2):
        latb = jnp.concatenate([lat_refs[2 * n][...].astype(BF16), lat_refs[2 * n + 1][...].astype(BF16)], axis=0)
        kpe = jnp.concatenate([kr_refs[2 * n][...].astype(BF16), kr_refs[2 * n + 1][...].astype(BF16)], axis=0)
        ksc = jnp.concatenate([ks_refs[2 * n][...], ks_refs[2 * n + 1][...]], axis=0)
        s = _dot(latb, qabs_t) + _dot(kpe, qpe_t)
        s = s * (_dot3(ksc, eye) * scale)
        s_parts.append(s.T[:B_HEADS])
        lat_pairs.append(latb)
    s = jnp.concatenate(s_parts, axis=1)
    m_old = m_ref[...]
    m_new = jnp.maximum(m_old, jnp.max(s, axis=-1, keepdims=True))
    alpha = jnp.exp(m_old - m_new)
    p = jnp.exp(s - m_new)
    l_ref[...] = alpha * l_ref[...] + jnp.sum(p, axis=-1, keepdims=True)
    pb = p.astype(BF16)
    acc = alpha * acc_ref[...]
    w = lat_pairs[0].shape[0]
    for n in range(pps // 2):
        acc = acc + _dot(pb[:, n * w:(n + 1) * w], lat_pairs[n])
    acc_ref[...] = acc
    m_ref[...] = m_new

    @pl.when(c == nc - 1)
    def _():
        latn = latn_ref[...]
        s_new = (jnp.sum(qabs_ref[...] * latn.astype(BF16).astype(F32), axis=-1, keepdims=True)
                 + jnp.sum(qpe_ref[...] * kpen_ref[...].astype(BF16).astype(F32), axis=-1, keepdims=True))
        s_new = s_new * (kscn_ref[...] * scale)
        m_f = jnp.maximum(m_ref[...], s_new)
        a = jnp.exp(m_ref[...] - m_f)
        p_new = jnp.exp(s_new - m_f)
        den = a * l_ref[...] + p_new
        o_ref[...] = (a * acc_ref[...] + p_new * latn) / den


def _mla_decode(page_table, c_lat, c_kr, c_ks, qabs_t, qpe_t, qabs, qpe, latn, kpen, kscn, pages_per_step=16):
    nseq, n_pages = page_table.shape
    page = c_lat.shape[1]
    pps = pages_per_step
    assert n_pages % pps == 0 and pps % 2 == 0
    eye = jnp.eye(B_HEADS, LANES, dtype=F32)

    def page_spec(w):
        return [pl.BlockSpec((None, page, w), lambda b, c, pt, p=p: (pt[b, c * pps + p], 0, 0)) for p in range(pps)]

    seq3 = lambda r, w: pl.BlockSpec((None, r, w), lambda b, c, pt: (b, 0, 0))
    grid_spec = pltpu.PrefetchScalarGridSpec(
        num_scalar_prefetch=1,
        grid=(nseq, n_pages // pps),
        in_specs=page_spec(KV_LORA) + page_spec(ROPE_DIM) + page_spec(B_HEADS)
                 + [seq3(KV_LORA, LANES), seq3(ROPE_DIM, LANES), pl.BlockSpec((B_HEADS, LANES), lambda b, c, pt: (0, 0)),
                    seq3(B_HEADS, KV_LORA), seq3(B_HEADS, ROPE_DIM), seq3(1, KV_LORA), seq3(1, ROPE_DIM),
                    seq3(B_HEADS, 1)],
        out_specs=seq3(B_HEADS, KV_LORA),
        scratch_shapes=[pltpu.VMEM((B_HEADS, 1), F32), pltpu.VMEM((B_HEADS, 1), F32),
                        pltpu.VMEM((B_HEADS, KV_LORA), F32)],
    )
    return pl.pallas_call(
        functools.partial(_mla_decode_kernel, pages_per_step=pps),
        grid_spec=grid_spec,
        out_shape=jax.ShapeDtypeStruct((nseq, B_HEADS, KV_LORA), F32),
        compiler_params=_cparams(("parallel", "arbitrary")),
        name="mla_decode",
    )(page_table, *([c_lat] * pps), *([c_kr] * pps), *([c_ks] * pps), qabs_t, qpe_t, eye, qabs, qpe, latn, kpen, kscn)


def _head_nt_kernel(a_ref, w_ref, o_ref):
    o_ref[...] = _dot_nt(a_ref[...], w_ref[...])


def _absorb_q(qm_s, wuk):
    nseq = qm_s.shape[0]
    return pl.pallas_call(
        _head_nt_kernel,
        grid=(B_HEADS,),
        in_specs=[pl.BlockSpec((nseq, NOPE_DIM), lambda h: (0, 2 * h)),
                  pl.BlockSpec((KV_LORA, NOPE_DIM), lambda h: (0, h))],
        out_specs=pl.BlockSpec((None, nseq, KV_LORA), lambda h: (h, 0, 0)),
        out_shape=jax.ShapeDtypeStruct((B_HEADS, nseq, KV_LORA), F32),
        compiler_params=_cparams(("parallel",)),
        name="absorb_q",
    )(qm_s, wuk)


def _head_nn_kernel(a_ref, w_ref, o_ref):
    o_ref[...] = _dot(a_ref[...].astype(BF16), w_ref[...]).astype(o_ref.dtype)


def _expand_o(o_lat_h, wuv):
    _, nseq, _ = o_lat_h.shape
    return pl.pallas_call(
        _head_nn_kernel,
        grid=(B_HEADS,),
        in_specs=[pl.BlockSpec((None, nseq, KV_LORA), lambda h: (h, 0, 0)),
                  pl.BlockSpec((KV_LORA, V_DIM), lambda h: (0, h))],
        out_specs=pl.BlockSpec((nseq, V_DIM), lambda h: (0, h)),
        out_shape=jax.ShapeDtypeStruct((nseq, B_HEADS * V_DIM), BF16),
        compiler_params=_cparams(("parallel",)),
        name="expand_o",
    )(o_lat_h, wuv)


def _merge_kernel(oa_ref, ob_ref, ga_ref, gb_ref, wa_ref, wb_ref, o_ref):
    a = jax.nn.sigmoid(ga_ref[...]) * _dot(oa_ref[...], wa_ref[...])
    b = jax.nn.sigmoid(gb_ref[...]) * _dot(ob_ref[...], wb_ref[...])
    o_ref[...] = (a + b).astype(o_ref.dtype)


def _merge(oa, ob, proj, wa, wb, tm):
    n = oa.shape[0]
    d = wa.shape[1]
    return pl.pallas_call(
        _merge_kernel,
        grid=(n // tm,),
        in_specs=[pl.BlockSpec((tm, oa.shape[1]), lambda i: (i, 0)), pl.BlockSpec((tm, ob.shape[1]), lambda i: (i, 0)),
                  pl.BlockSpec((tm, d), lambda i: (i, COL_GA // d)), pl.BlockSpec((tm, d), lambda i: (i, COL_GB // d)),
                  pl.BlockSpec(wa.shape, lambda i: (0, 0)), pl.BlockSpec(wb.shape, lambda i: (0, 0))],
        out_specs=pl.BlockSpec((tm, d), lambda i: (i, 0)),
        out_shape=jax.ShapeDtypeStruct((n, d), BF16),
        compiler_params=_cparams(("parallel",)),
        name="merge",
    )(oa, ob, proj, proj, wa, wb)


def _outproj_router_kernel(t_ref, x_ref, wo_ref, g_ref, wr_ref, rb_ref, h_ref, hn_ref, ti_ref, tw_ref):
    h = x_ref[...] + _dot(t_ref[...], wo_ref[...])
    h_ref[...] = h
    inv = lax.rsqrt(jnp.mean(h * h, axis=-1, keepdims=True) + EPS)
    hn = (h * inv * g_ref[...]).astype(BF16)
    hn_ref[...] = hn
    scores = jax.nn.sigmoid(_dot(hn, wr_ref[...]))
    lane = _lane_iota(scores.shape)
    sel = jnp.where(lane < N_EXPERTS, scores + rb_ref[...], NEG)
    ti = jnp.zeros(scores.shape, jnp.int32)
    tw = jnp.zeros(scores.shape, F32)
    for r in range(TOP_K):
        mx = jnp.max(sel, axis=-1, keepdims=True)
        idx = jnp.min(jnp.where(sel == mx, lane, LANES), axis=-1, keepdims=True)
        pick = lane == idx
        w = jnp.sum(jnp.where(pick, scores, 0.0), axis=-1, keepdims=True)
        ti = jnp.where(lane == r, idx, ti)
        tw = jnp.where(lane == r, w, tw)
        sel = jnp.where(pick, -3e38, sel)
    tw = tw / jnp.sum(tw, axis=-1, keepdims=True) * ROUTED_SCALE
    ti_ref[...] = ti[:, :8]
    tw_ref[...] = tw[:, :8]


def _outproj_router(t, x, wo, g, wr, rb, tm):
    n, d = x.shape
    row = lambda w: pl.BlockSpec((tm, w), lambda i: (i, 0))
    const = lambda a: pl.BlockSpec(a.shape, lambda i: (0,) * a.ndim)
    return pl.pallas_call(
        _outproj_router_kernel,
        grid=(n // tm,),
        in_specs=[row(d), row(d), const(wo), const(g), const(wr), const(rb)],
        out_specs=[row(d), row(d), row(8), row(8)],
        out_shape=[jax.ShapeDtypeStruct((n, d), F32), jax.ShapeDtypeStruct((n, d), BF16),
                   jax.ShapeDtypeStruct((n, 8), jnp.int32), jax.ShapeDtypeStruct((n, 8), F32)],
        compiler_params=_cparams(("parallel",)),
        name="outproj_router",
    )(t, x, wo, g, wr, rb)


def _experts_kernel(be_ref, nv_ref, x_ref, wg_ref, wu_ref, wd_ref, o_ref):
    del be_ref
    b = pl.program_id(0)

    @pl.when(b < nv_ref[0])
    def _():
        x = x_ref[...]
        hid = jax.nn.silu(_dot(x, wg_ref[...].astype(BF16))) * _dot(x, wu_ref[...].astype(BF16))
        o_ref[...] = _dot(hid.astype(BF16), wd_ref[...].astype(BF16))

    @pl.when(b >= nv_ref[0])
    def _():
        o_ref[...] = jnp.zeros(o_ref.shape, o_ref.dtype)


def _experts(blk_expert, n_valid, xs, wg, wu, wd, tm):
    rows, d = xs.shape
    de = wg.shape[2]
    grid_spec = pltpu.PrefetchScalarGridSpec(
        num_scalar_prefetch=2,
        grid=(rows // tm,),
        in_specs=[pl.BlockSpec((tm, d), lambda b, be, nv: (b, 0)),
                  pl.BlockSpec((None, d, de), lambda b, be, nv: (be[b], 0, 0)),
                  pl.BlockSpec((None, d, de), lambda b, be, nv: (be[b], 0, 0)),
                  pl.BlockSpec((None, de, d), lambda b, be, nv: (be[b], 0, 0))],
        out_specs=pl.BlockSpec((tm, d), lambda b, be, nv: (b, 0)),
    )
    return pl.pallas_call(
        _experts_kernel,
        grid_spec=grid_spec,
        out_shape=jax.ShapeDtypeStruct((rows, d), F32),
        compiler_params=_cparams(("arbitrary",)),
        name="experts",
    )(blk_expert, n_valid, xs, wg, wu, wd)


def _shared_combine_kernel(hn_ref, h_ref, r_ref, wg_ref, wu_ref, wd_ref, o_ref):
    x = hn_ref[...]
    hid = jax.nn.silu(_dot(x, wg_ref[...])) * _dot(x, wu_ref[...])
    o_ref[...] = h_ref[...] + r_ref[...] + _dot(hid.astype(BF16), wd_ref[...])


def _shared_combine(hn, h, routed, wg, wu, wd, tm):
    n, d = h.shape
    row = pl.BlockSpec((tm, d), lambda i: (i, 0))
    const = lambda a: pl.BlockSpec(a.shape, lambda i: (0, 0))
    return pl.pallas_call(
        _shared_combine_kernel,
        grid=(n // tm,),
        in_specs=[row, row, row, const(wg), const(wu), const(wd)],
        out_specs=row,
        out_shape=jax.ShapeDtypeStruct((n, d), F32),
        compiler_params=_cparams(("parallel",)),
        name="shared_combine",
    )(hn, h, routed, wg, wu, wd)


def _rel_bucket(dist):
    dist = jnp.maximum(dist, 0)
    max_exact = REL_BUCKETS // 2
    d = jnp.maximum(dist, 1).astype(F32)
    large = max_exact + (jnp.log(d / max_exact) / math.log(REL_MAX_DIST / max_exact)
                         * (REL_BUCKETS - max_exact)).astype(jnp.int32)
    large = jnp.minimum(large, REL_BUCKETS - 1)
    return jnp.where(dist < max_exact, dist, large)


def _rope_table(pos):
    half = ROPE_DIM // 2
    freqs = jnp.power(ROPE_THETA, -jnp.arange(half, dtype=F32) / half)
    ang = pos.astype(F32)[:, None] * freqs
    cos, sin = jnp.cos(ang), jnp.sin(ang)
    return jnp.concatenate([cos, cos, -sin, sin], axis=-1)


def _prepare(w_in, a_q_norm, a_k_norm, b_q_a_norm, b_q_up, b_kv_a_norm, b_w_uk, b_w_uv, b_q_norm, b_k_norm):
    d = w_in.shape[0]
    swap = np.concatenate([np.arange(ROPE_DIM // 2, ROPE_DIM), np.arange(0, ROPE_DIM // 2)])
    o = np.cumsum([0, A_Q_W, A_KV_W, A_KV_W, Q_LORA, KV_LORA, ROPE_DIM, d, d])
    aq, ak, av, bq, bkv, bkr, ga, gb = [w_in[:, o[i]:o[i + 1]] for i in range(8)]
    wcat = jnp.concatenate([ga, gb, aq, ak, av, bq, bkv, bkr, bkr[:, swap], jnp.zeros((d, LANES), F32)], axis=1)
    assert wcat.shape[1] == W_CAT_COLS
    q_pe_w = b_q_up[:, :, NOPE_DIM:]
    wq = jnp.concatenate([b_q_up[:, :, :NOPE_DIM], q_pe_w, q_pe_w[:, :, swap]], axis=-1)
    gq_head = jnp.concatenate([b_q_norm[:NOPE_DIM] * b_k_norm[:NOPE_DIM], b_q_norm[NOPE_DIM:], b_q_norm[NOPE_DIM:][swap]])
    return {
        "wcat": wcat.astype(BF16),
        "wq": wq.reshape(Q_LORA, B_HEADS * MLA_HEAD_PAD).astype(BF16),
        "wuk": b_w_uk.reshape(KV_LORA, B_HEADS * NOPE_DIM).astype(BF16),
        "wuv": b_w_uv.reshape(KV_LORA, B_HEADS * V_DIM).astype(BF16),
        "gaq": jnp.tile(a_q_norm, A_HEADS)[None, :],
        "gak": jnp.tile(a_k_norm, A_KV_HEADS)[None, :],
        "gbq": b_q_a_norm[None, :],
        "gbkv": b_kv_a_norm[None, :],
        "gq": jnp.tile(gq_head, B_HEADS)[None, :],
        "gk": jnp.concatenate([b_k_norm[NOPE_DIM:], b_k_norm[NOPE_DIM:][swap]])[None, :],
    }


def _bias_tables(rel_bias):
    dist = jnp.arange(2 * MOBA_BLOCK, dtype=jnp.int32)
    tbl = rel_bias[:, _rel_bucket(dist)] - rel_bias[:, REL_BUCKETS - 1:]
    r = jnp.arange(MOBA_BLOCK)[:, None]
    c = jnp.arange(MOBA_BLOCK)[None, :]
    tdiag = jnp.where(c <= r, tbl[:, jnp.maximum(r - c, 0)], NEG)
    tprev = tbl[:, MOBA_BLOCK + r - c]
    tlast = tbl[:, MOBA_BLOCK - jnp.arange(MOBA_BLOCK)]
    return tdiag, tprev, tlast, tbl[:, :1]


def _dispatch(topi, tm):
    n = topi.shape[0]
    a = n * TOP_K
    e_flat = topi[:, :TOP_K].reshape(a)
    oh = (e_flat[:, None] == jnp.arange(N_EXPERTS, dtype=jnp.int32)[None, :]).astype(jnp.int32)
    csum = jnp.cumsum(oh, axis=0)
    counts = csum[-1]
    rank = jnp.sum(oh * csum, axis=1) - 1
    padded = (counts + tm - 1) // tm * tm
    pad_end = jnp.cumsum(padded)
    pad_start = pad_end - padded
    dest = jnp.sum(oh * pad_start[None, :], axis=1) + rank
    nb = -(-a // tm) + N_EXPERTS
    tok = jnp.arange(a, dtype=jnp.int32) // TOP_K
    row_tok = jnp.zeros((nb * tm,), jnp.int32).at[dest].set(tok)
    blk_expert = jnp.minimum(jnp.searchsorted(pad_end, jnp.arange(nb, dtype=jnp.int32) * tm, side="right"),
                             N_EXPERTS - 1).astype(jnp.int32)
    n_valid = (pad_end[-1:] // tm).astype(jnp.int32)
    return dest.reshape(n, TOP_K), row_tok, blk_expert, n_valid


def _tile(n, pref):
    t = pref
    while n % t:
        t //= 2
    return t


def kernel(x_prompt, x_sample, cache_moba_k, cache_moba_v, cache_mla_latent, cache_mla_krope, cache_mla_kscale,
           page_table, rel_bias, attn_norm, w_in, a_q_norm, a_k_norm, b_q_a_norm, b_q_up, b_kv_a_norm, b_w_uk,
           b_w_uv, b_q_norm, b_k_norm, w_branch_a, w_branch_b, w_out, ffn_norm, w_router, router_bias,
           w_e_gate, w_e_up, w_e_down, w_s_gate, w_s_up, w_s_down):
    depth = w_in.shape[0]
    assert depth == 1
    b_p, s_p, d = x_prompt.shape
    b_d, s_d, _ = x_sample.shape
    assert b_p == 1 and s_d == 1 and s_p % MOBA_BLOCK == 0
    pool, page = cache_moba_k.shape[1], cache_moba_k.shape[2]
    n_pages = page_table.shape[1]
    past = n_pages * page
    assert past % MOBA_BLOCK == 0 and MOBA_BLOCK % page == 0
    l = 0

    prm = _prepare(w_in[l], a_q_norm[l], a_k_norm[l], b_q_a_norm[l], b_q_up[l], b_kv_a_norm[l], b_w_uk[l],
                   b_w_uv[l], b_q_norm[l], b_k_norm[l])
    tdiag, tprev, tlast, bias0 = _bias_tables(rel_bias)
    g_attn = attn_norm[l][None, :]
    wa = w_branch_a[l].astype(BF16)
    wb = w_branch_b[l].astype(BF16)
    wo = w_out[l].astype(BF16)
    g_ffn = ffn_norm[l][None, :]
    wr = jnp.pad(w_router[l], ((0, 0), (0, LANES - N_EXPERTS))).astype(BF16)
    rb = jnp.pad(router_bias[l], (0, LANES - N_EXPERTS))[None, :]
    wsg, wsu, wsd = w_s_gate[l].astype(BF16), w_s_up[l].astype(BF16), w_s_down[l].astype(BF16)

    def token_front(x, pos, tm_front):
        n = x.shape[0]
        tm = _tile(n, 512)
        xn = _rmsnorm_bf16(x, g_attn, tm)
        proj = _matmul(xn, prm["wcat"], tm, W_CAT_COLS // 6)
        return proj, _front(proj, _rope_table(pos), prm, tm_front)

    def token_back(x, proj, oa, ob):
        n = x.shape[0]
        tm = _tile(n, 512)
        t = _merge(oa, ob, proj, wa, wb, tm)
        return _outproj_router(t, x, wo, g_ffn, wr, rb, _tile(n, 256))

    xp = x_prompt.reshape(s_p, d)
    proj_p, fp = token_front(xp, jnp.arange(s_p, dtype=jnp.int32), MOBA_BLOCK)
    qa, ka, va, kaug, vbf, kmean, qm, km, vm, lat, kpe, ksc = fp
    nblk = s_p // MOBA_BLOCK
    assert nblk <= LANES
    kmean_pad = jnp.pad(kmean.reshape(nblk, A_KV_HEADS, A_HEAD_DIM).transpose(1, 0, 2),
                        ((0, 0), (0, LANES - nblk), (0, 0)))
    tdiag_g = tdiag.reshape(A_KV_HEADS * A_GROUP, MOBA_BLOCK, MOBA_BLOCK)
    oa_p = _moba_prompt(qa, kaug, vbf, kmean_pad, tdiag_g, tprev)
    ob_p = _mla_prompt(qm, km, vm, _tile(s_p, 512))
    h_p, hn_p, ti_p, tw_p = token_back(xp, proj_p, oa_p, ob_p)

    xd = x_sample.reshape(b_d, d)
    pos_d = jnp.full((b_d,), past, jnp.int32)
    proj_d, fd = token_front(xd, pos_d, _tile(b_d, 256))
    qa_d, ka_d, va_d, _, _, _, qm_d, _, _, lat_d, kpe_d, ksc_d = fd
    ck = cache_moba_k[l].reshape(pool, page * A_KV_HEADS, A_HEAD_DIM)
    cv = cache_moba_v[l].reshape(pool, page * A_KV_HEADS, A_HEAD_DIM)
    rep = lambda t: jnp.repeat(t.reshape(b_d, A_KV_HEADS, A_HEAD_DIM), A_GROUP, axis=1)
    oa_d = _moba_decode(page_table, ck, cv, qa_d.reshape(b_d, A_HEADS, A_HEAD_DIM), rep(ka_d), rep(va_d),
                        tlast, bias0)
    oa_d = oa_d.reshape(b_d, A_Q_W).astype(BF16)

    qabs_h = _absorb_q(qm_d, prm["wuk"])
    qabs = qabs_h.transpose(1, 0, 2).astype(BF16)
    qabs_t = jnp.pad(qabs.transpose(0, 2, 1), ((0, 0), (0, 0), (0, LANES - B_HEADS)))
    qpe = qm_d.reshape(b_d, B_HEADS, MLA_HEAD_PAD)[:, :, NOPE_DIM:NOPE_DIM + ROPE_DIM]
    qpe_t = jnp.pad(qpe.transpose(0, 2, 1), ((0, 0), (0, 0), (0, LANES - B_HEADS)))
    o_lat = _mla_decode(page_table, cache_mla_latent[l], cache_mla_krope[l], cache_mla_kscale[l],
                        qabs_t, qpe_t, qabs.astype(F32), qpe.astype(F32),
                        lat_d[:, None, :], kpe_d[:, None, :], ksc_d[:, :, None])
    ob_d = _expand_o(o_lat.transpose(1, 0, 2), prm["wuv"])
    h_d, hn_d, ti_d, tw_d = token_back(xd, proj_d, oa_d, ob_d)

    n_all = s_p + b_d
    hn_all = jnp.concatenate([hn_p, hn_d], axis=0)
    ti_all = jnp.concatenate([ti_p, ti_d], axis=0)
    tw_all = jnp.concatenate([tw_p, tw_d], axis=0)
    tm_e = 256
    dest, row_tok, blk_expert, n_valid = _dispatch(ti_all, tm_e)
    xs = hn_all[row_tok]
    yb = _experts(blk_expert, n_valid, xs, w_e_gate[l], w_e_up[l], w_e_down[l], tm_e)
    routed = jnp.sum(yb[dest] * tw_all[:, :TOP_K, None], axis=1)
    y_p = _shared_combine(hn_p, h_p, routed[:s_p], wsg, wsu, wsd, _tile(s_p, 512))
    y_d = _shared_combine(hn_d, h_d, routed[s_p:], wsg, wsu, wsd, _tile(b_d, 512))
    del n_all

    y_prompt = y_p.reshape(b_p, s_p, d)
    y_sample = y_d.reshape(b_d, s_d, d)
    return (y_prompt, y_sample,
            ka.reshape(1, b_p, s_p, A_KV_HEADS, A_HEAD_DIM), va.reshape(1, b_p, s_p, A_KV_HEADS, A_HEAD_DIM),
            lat.reshape(1, b_p, s_p, KV_LORA), kpe.reshape(1, b_p, s_p, ROPE_DIM), ksc.reshape(1, b_p, s_p, B_HEADS),
            ka_d.reshape(1, b_d, s_d, A_KV_HEADS, A_HEAD_DIM), va_d.reshape(1, b_d, s_d, A_KV_HEADS, A_HEAD_DIM),
            lat_d.reshape(1, b_d, s_d, KV_LORA), kpe_d.reshape(1, b_d, s_d, ROPE_DIM),
            ksc_d.reshape(1, b_d, s_d, B_HEADS))
```

```python
import functools
import math

import numpy as np
import jax
import jax.numpy as jnp
from jax import lax
from jax.experimental import pallas as pl
from jax.experimental.pallas import tpu as pltpu

F32 = jnp.float32
BF16 = jnp.bfloat16

A_HEADS = 8
A_KV_HEADS = 2
A_HEAD_DIM = 128
A_GROUP = A_HEADS // A_KV_HEADS
MOBA_BLOCK = 256
MOBA_TOPK = 3
B_HEADS = 8
Q_LORA = 512
KV_LORA = 512
NOPE_DIM = 128
ROPE_DIM = 64
V_DIM = 128
B_QK_DIM = NOPE_DIM + ROPE_DIM
ROPE_THETA = 10000.0
REL_BUCKETS = 32
REL_MAX_DIST = 128
N_EXPERTS = 64
TOP_K = 6
ROUTED_SCALE = 1.0
EPS = 1e-6

LANES = 128
MLA_HEAD_PAD = 256
NEG = -1e30
VMEM_LIMIT = 56 * 1024 * 1024

A_Q_W = A_HEADS * A_HEAD_DIM
A_KV_W = A_KV_HEADS * A_HEAD_DIM

COL_GA = 0
COL_GB = 2048
COL_AQ = 4096
COL_AK = 5120
COL_AV = 5376
COL_BQ = 5632
COL_BKV = 6144
COL_BKR = 6656
W_CAT_COLS = 6912


def _cparams(sem):
    return pltpu.CompilerParams(dimension_semantics=sem, vmem_limit_bytes=VMEM_LIMIT)


def _dot(a, b):
    return jnp.dot(a, b, preferred_element_type=F32)


def _dot_nt(a, b):
    return lax.dot_general(a, b, (((1,), (1,)), ((), ())), preferred_element_type=F32)


def _split_bf16(a):
    hi = a.astype(BF16)
    lo = (a - hi.astype(F32)).astype(BF16)
    return hi, lo


def _dot3(a, b):
    ah, al = _split_bf16(a)
    bh, bl = _split_bf16(b)
    return _dot(ah, bh) + (_dot(ah, bl) + _dot(al, bh))


def _lane_iota(shape):
    return lax.broadcasted_iota(jnp.int32, shape, len(shape) - 1)


def _rmsnorm_kernel(x_ref, g_ref, o_ref):
    x = x_ref[...]
    inv = lax.rsqrt(jnp.mean(x * x, axis=-1, keepdims=True) + EPS)
    o_ref[...] = (x * inv * g_ref[...]).astype(o_ref.dtype)


def _rmsnorm_bf16(x, g, tm):
    n, d = x.shape
    return pl.pallas_call(
        _rmsnorm_kernel,
        grid=(n // tm,),
        in_specs=[pl.BlockSpec((tm, d), lambda i: (i, 0)), pl.BlockSpec((1, d), lambda i: (0, 0))],
        out_specs=pl.BlockSpec((tm, d), lambda i: (i, 0)),
        out_shape=jax.ShapeDtypeStruct((n, d), BF16),
        compiler_params=_cparams(("parallel",)),
        name="rmsnorm",
    )(x, g)


def _matmul_kernel(a_ref, b_ref, o_ref):
    o_ref[...] = _dot(a_ref[...], b_ref[...])


def _matmul(a, b, tm, tn):
    m, k = a.shape
    _, n = b.shape
    return pl.pallas_call(
        _matmul_kernel,
        grid=(n // tn, m // tm),
        in_specs=[pl.BlockSpec((tm, k), lambda j, i: (i, 0)), pl.BlockSpec((k, tn), lambda j, i: (0, j))],
        out_specs=pl.BlockSpec((tm, tn), lambda j, i: (i, j)),
        out_shape=jax.ShapeDtypeStruct((m, n), F32),
        compiler_params=_cparams(("parallel", "parallel")),
        name="in_proj",
    )(a, b)


def _front_kernel(aq_ref, ak_ref, av_ref, bq_ref, bkv_ref, bkr_ref, cs_ref,
                  gaq_ref, gak_ref, gbq_ref, gbkv_ref, gq_ref, gk_ref,
                  wq_ref, wuk_ref, wuv_ref,
                  qa_ref, ka_ref, va_ref, kaug_ref, vbf_ref, kmean_ref,
                  qm_ref, km_ref, vm_ref, lat_ref, kpe_ref, ksc_ref, *, tm):
    pid = pl.program_id(0)
    lane = _lane_iota((tm, LANES))
    lo = lane < ROPE_DIM
    cs = cs_ref[...]

    def head_norm(x, g):
        inv = lax.rsqrt(jnp.mean(x * x, axis=-1, keepdims=True) + EPS)
        return x * inv * g

    for h in range(A_HEADS):
        sl = slice(h * A_HEAD_DIM, (h + 1) * A_HEAD_DIM)
        qa_ref[:, sl] = head_norm(aq_ref[:, sl], gaq_ref[:, sl])
    row = lax.broadcasted_iota(jnp.int32, (tm, LANES), 0) + pid * tm
    onehot = jnp.where(lane == row // MOBA_BLOCK, 1.0, 0.0).astype(BF16)
    for g in range(A_KV_HEADS):
        sl = slice(g * A_HEAD_DIM, (g + 1) * A_HEAD_DIM)
        k = head_norm(ak_ref[:, sl], gak_ref[:, sl])
        ka_ref[:, sl] = k
        kmean_ref[0, :, sl] = jnp.mean(k, axis=0, keepdims=True)
        kaug_ref[:, 2 * g * LANES:(2 * g + 1) * LANES] = k.astype(BF16)
        kaug_ref[:, (2 * g + 1) * LANES:(2 * g + 2) * LANES] = onehot
    av = av_ref[...]
    va_ref[...] = av
    ones = jnp.ones((tm, LANES), BF16)
    for g in range(A_KV_HEADS):
        vbf_ref[:, 2 * g * LANES:(2 * g + 1) * LANES] = av[:, g * A_HEAD_DIM:(g + 1) * A_HEAD_DIM].astype(BF16)
        vbf_ref[:, (2 * g + 1) * LANES:(2 * g + 2) * LANES] = ones

    qc = head_norm(bq_ref[...], gbq_ref[...]).astype(BF16)
    qraw = _dot(qc, wq_ref[...])
    for h in range(B_HEADS):
        a = qraw[:, h * MLA_HEAD_PAD:h * MLA_HEAD_PAD + LANES]
        b = qraw[:, h * MLA_HEAD_PAD + LANES:(h + 1) * MLA_HEAD_PAD]
        ssq = jnp.sum(a * a, axis=-1, keepdims=True) + jnp.sum(jnp.where(lo, b * b, 0.0), axis=-1, keepdims=True)
        inv = lax.rsqrt(ssq / B_QK_DIM + EPS)
        qm_ref[:, h * MLA_HEAD_PAD:h * MLA_HEAD_PAD + LANES] = (
            a * inv * gq_ref[:, h * MLA_HEAD_PAD:h * MLA_HEAD_PAD + LANES]).astype(BF16)
        t = b * inv * gq_ref[:, h * MLA_HEAD_PAD + LANES:(h + 1) * MLA_HEAD_PAD] * cs
        pe = jnp.where(lo, t + pltpu.roll(t, ROPE_DIM, axis=1), 0.0)
        qm_ref[:, h * MLA_HEAD_PAD + LANES:(h + 1) * MLA_HEAD_PAD] = pe.astype(BF16)

    lat = head_norm(bkv_ref[...], gbkv_ref[...])
    lat_ref[...] = lat
    latb = lat.astype(BF16)
    knope = _dot(latb, wuk_ref[...])
    vb = _dot(latb, wuv_ref[...]).astype(BF16)
    for h in range(B_HEADS):
        vm_ref[:, 2 * h * LANES:(2 * h + 1) * LANES] = vb[:, h * V_DIM:(h + 1) * V_DIM]
        vm_ref[:, (2 * h + 1) * LANES:(2 * h + 2) * LANES] = ones
    bkr = bkr_ref[...]
    rss = jnp.sum(jnp.where(lo, bkr * bkr, 0.0), axis=-1, keepdims=True)
    t = bkr * gk_ref[...] * cs
    kpe = jnp.where(lo, t + pltpu.roll(t, ROPE_DIM, axis=1), 0.0)
    kpe_ref[...] = kpe[:, :ROPE_DIM]
    ksw = jnp.zeros((tm, LANES), F32)
    scale = B_QK_DIM ** -0.5
    for h in range(B_HEADS):
        kh = knope[:, h * NOPE_DIM:(h + 1) * NOPE_DIM]
        ssq = jnp.sum(kh * kh, axis=-1, keepdims=True) + rss
        ks = lax.rsqrt(ssq / B_QK_DIM + EPS)
        ksw = jnp.where(lane == h, ks, ksw)
        km_ref[:, h * MLA_HEAD_PAD:h * MLA_HEAD_PAD + LANES] = (kh * (ks * scale)).astype(BF16)
        km_ref[:, h * MLA_HEAD_PAD + LANES:(h + 1) * MLA_HEAD_PAD] = (kpe * (ks * scale)).astype(BF16)
    ksc_ref[...] = ksw[:, :B_HEADS]


def _front(proj, cs, prm, tm):
    n = proj.shape[0]
    nt = n // tm
    row = lambda w, c: pl.BlockSpec((tm, w), lambda i, c=c, w=w: (i, c // w))
    const = lambda a: pl.BlockSpec(a.shape, lambda i: (0,) * a.ndim)
    consts = [prm["gaq"], prm["gak"], prm["gbq"], prm["gbkv"], prm["gq"], prm["gk"], prm["wq"], prm["wuk"], prm["wuv"]]
    out_shapes = [
        ((n, A_Q_W), F32), ((n, A_KV_W), F32), ((n, A_KV_W), F32),
        ((n, 2 * A_KV_W), BF16), ((n, 2 * A_KV_W), BF16),
        ((nt, 1, A_KV_W), F32),
        ((n, B_HEADS * MLA_HEAD_PAD), BF16), ((n, B_HEADS * MLA_HEAD_PAD), BF16),
        ((n, B_HEADS * 2 * V_DIM), BF16),
        ((n, KV_LORA), F32), ((n, ROPE_DIM), F32), ((n, B_HEADS), F32),
    ]
    out_specs = []
    for shp, _ in out_shapes:
        if len(shp) == 3:
            out_specs.append(pl.BlockSpec((1, 1, shp[2]), lambda i: (i, 0, 0)))
        else:
            out_specs.append(pl.BlockSpec((tm, shp[1]), lambda i: (i, 0)))
    return pl.pallas_call(
        functools.partial(_front_kernel, tm=tm),
        grid=(nt,),
        in_specs=[row(A_Q_W, COL_AQ), row(A_KV_W, COL_AK), row(A_KV_W, COL_AV), row(Q_LORA, COL_BQ),
                  row(KV_LORA, COL_BKV), row(LANES, COL_BKR), pl.BlockSpec((tm, LANES), lambda i: (i, 0))]
                 + [const(a) for a in consts],
        out_specs=out_specs,
        out_shape=[jax.ShapeDtypeStruct(s, d) for s, d in out_shapes],
        compiler_params=_cparams(("parallel",)),
        name="front",
    )(proj, proj, proj, proj, proj, proj, cs, *consts)


def _top_blocks(gate, valid, lane):
    g = jnp.where(valid, gate, NEG)
    selv = jnp.zeros(gate.shape, F32)
    for _ in range(MOBA_TOPK):
        mx = jnp.max(g, axis=-1, keepdims=True)
        idx = jnp.min(jnp.where(g == mx, lane, LANES), axis=-1, keepdims=True)
        pick = lane == idx
        selv = jnp.where(pick, 1.0, selv)
        g = jnp.where(pick, -3e38, g)
    return jnp.where(valid, selv, 0.0)


def _rep(x, width):
    return x if width == LANES else jnp.concatenate([x] * (width // LANES), axis=1)


def _flash_first(s, v, m_ref, acc_ref, idx):
    m = jnp.max(s, axis=-1, keepdims=True)
    p = jnp.exp(s - m)
    m_ref[idx] = jnp.broadcast_to(m, (s.shape[0], LANES))
    acc_ref[idx] = _dot(p.astype(BF16), v)


def _flash_next(s, v, m_ref, acc_ref, idx):
    m_old = m_ref[idx]
    m_new = jnp.maximum(m_old, jnp.max(s, axis=-1, keepdims=True))
    alpha = jnp.exp(m_old - m_new)
    p = jnp.exp(s - _rep(m_new, s.shape[-1]))
    acc_ref[idx] = _rep(alpha, 2 * LANES) * acc_ref[idx] + _dot(p.astype(BF16), v)
    m_ref[idx] = m_new


def _flash_out(acc):
    return acc[:, :LANES] / acc[:, LANES:]


def _moba_prompt_kernel(q_ref, kaug_ref, v_ref, kmean_ref, tdiag_ref, tprev_ref, o_ref,
                        qaug_ref, m_ref, acc_ref):
    i = pl.program_id(1)
    blk = MOBA_BLOCK
    lane = _lane_iota((blk, LANES))
    valid = lane < i
    kmean = kmean_ref[...]
    scale = A_HEAD_DIM ** -0.5

    diag0 = pl.multiple_of(i * blk, blk)
    kd = kaug_ref[pl.ds(diag0, blk), :]
    vd = v_ref[pl.ds(diag0, blk), :]
    for hh in range(A_GROUP):
        q = q_ref[:, hh * A_HEAD_DIM:(hh + 1) * A_HEAD_DIM]
        qh, ql = _split_bf16(q)
        kh, kl = _split_bf16(kmean)
        gate = _dot_nt(qh, kh) + (_dot_nt(qh, kl) + _dot_nt(ql, kh))
        selv = _top_blocks(gate, valid, lane)
        on = jnp.where(lane == i, 1.0, selv)
        qaug_ref[hh, :, :LANES] = (q * scale).astype(BF16)
        qaug_ref[hh, :, LANES:] = jnp.where(on > 0.0, 0.0, NEG).astype(BF16)
        _flash_first(_dot_nt(qaug_ref[hh], kd) + tdiag_ref[hh], vd, m_ref, acc_ref, hh)

    def step(j0, bias_ref):
        kj = kaug_ref[pl.ds(j0, blk), :]
        vj = v_ref[pl.ds(j0, blk), :]
        for hh in range(A_GROUP):
            s = _dot_nt(qaug_ref[hh], kj)
            if bias_ref is not None:
                s = s + bias_ref[hh]
            _flash_next(s, vj, m_ref, acc_ref, hh)

    @pl.when(i >= 1)
    def _():
        step(pl.multiple_of((i - 1) * blk, blk), tprev_ref)

    def far(j, carry):
        step(pl.multiple_of(j * blk, blk), None)
        return carry

    lax.fori_loop(0, jnp.maximum(i - 1, 0), far, 0)

    for hh in range(A_GROUP):
        o_ref[:, hh * A_HEAD_DIM:(hh + 1) * A_HEAD_DIM] = _flash_out(acc_ref[hh]).astype(o_ref.dtype)


def _moba_prompt(qa, kaug, vbf, kmean_pad, tdiag, tprev):
    s = qa.shape[0]
    nblk = s // MOBA_BLOCK
    gw = A_GROUP * A_HEAD_DIM
    return pl.pallas_call(
        _moba_prompt_kernel,
        grid=(A_KV_HEADS, nblk),
        in_specs=[
            pl.BlockSpec((MOBA_BLOCK, gw), lambda g, i: (i, g)),
            pl.BlockSpec((s, 2 * LANES), lambda g, i: (0, g)),
            pl.BlockSpec((s, 2 * LANES), lambda g, i: (0, g)),
            pl.BlockSpec((None, LANES, A_HEAD_DIM), lambda g, i: (g, 0, 0)),
            pl.BlockSpec((A_GROUP, MOBA_BLOCK, MOBA_BLOCK), lambda g, i: (g, 0, 0)),
            pl.BlockSpec((A_GROUP, MOBA_BLOCK, MOBA_BLOCK), lambda g, i: (g, 0, 0)),
        ],
        out_specs=pl.BlockSpec((MOBA_BLOCK, gw), lambda g, i: (i, g)),
        out_shape=jax.ShapeDtypeStruct((s, A_Q_W), BF16),
        scratch_shapes=[
            pltpu.VMEM((A_GROUP, MOBA_BLOCK, 2 * LANES), BF16),
            pltpu.VMEM((A_GROUP, MOBA_BLOCK, LANES), F32),
            pltpu.VMEM((A_GROUP, MOBA_BLOCK, 2 * LANES), F32),
        ],
        compiler_params=_cparams(("parallel", "parallel")),
        name="moba_prompt",
    )(qa, kaug, vbf, kmean_pad, tdiag, tprev)


def _mla_prompt_kernel(q_ref, k_ref, v_ref, o_ref, m_ref, acc_ref, *, tq, parts):
    qi = pl.program_id(1)
    rows = tq // parts
    d0 = pl.multiple_of(qi * tq, tq)
    kd = k_ref[pl.ds(d0, tq), :]
    vd = v_ref[pl.ds(d0, tq), :]
    c = lax.broadcasted_iota(jnp.int32, (rows, tq), 1)
    for a in range(parts):
        r = lax.broadcasted_iota(jnp.int32, (rows, tq), 0) + a * rows
        s = jnp.where(c <= r, _dot_nt(q_ref[a], kd), NEG)
        _flash_first(s, vd, m_ref, acc_ref, a)

    def body(j, carry):
        j0 = pl.multiple_of(j * tq, tq)
        kj = k_ref[pl.ds(j0, tq), :]
        vj = v_ref[pl.ds(j0, tq), :]
        for a in range(parts):
            _flash_next(_dot_nt(q_ref[a], kj), vj, m_ref, acc_ref, a)
        return carry

    lax.fori_loop(0, qi, body, 0)
    for a in range(parts):
        o_ref[a] = _flash_out(acc_ref[a]).astype(o_ref.dtype)


def _mla_prompt(qm, km, vm, tq, parts):
    s = qm.shape[0]
    rows = tq // parts
    out = pl.pallas_call(
        functools.partial(_mla_prompt_kernel, tq=tq, parts=parts),
        grid=(B_HEADS, s // tq),
        in_specs=[
            pl.BlockSpec((parts, rows, MLA_HEAD_PAD), lambda h, i: (i, 0, h)),
            pl.BlockSpec((s, MLA_HEAD_PAD), lambda h, i: (0, h)),
            pl.BlockSpec((s, 2 * V_DIM), lambda h, i: (0, h)),
        ],
        out_specs=pl.BlockSpec((parts, rows, V_DIM), lambda h, i: (i, 0, h)),
        out_shape=jax.ShapeDtypeStruct((s // rows, rows, B_HEADS * V_DIM), BF16),
        scratch_shapes=[pltpu.VMEM((parts, rows, LANES), F32), pltpu.VMEM((parts, rows, 2 * LANES), F32)],
        compiler_params=_cparams(("parallel", "parallel")),
        name="mla_prompt",
    )(qm.reshape(s // rows, rows, B_HEADS * MLA_HEAD_PAD), km, vm)
    return out.reshape(s, B_HEADS * V_DIM)


def _moba_gate_kernel(pt_ref, *refs, pages_per_step, n_blocks):
    del pt_ref
    pps = pages_per_step
    k_refs = refs[:pps]
    q_ref, sel_ref, g_all = refs[pps:]
    c = pl.program_id(1)
    nc = pl.num_programs(1)
    page = k_refs[0].shape[0] // A_KV_HEADS
    ppb = MOBA_BLOCK // page
    bps = pps // ppb
    lane = _lane_iota((A_HEADS, LANES))
    head = lax.broadcasted_iota(jnp.int32, (A_HEADS, LANES), 0)
    first_group = head < A_GROUP
    q = q_ref[...]

    @pl.when(c == 0)
    def _():
        g_all[...] = jnp.full((A_HEADS, LANES), NEG, F32)

    g_new = g_all[...]
    for nb in range(bps):
        ksum = []
        for g in range(A_KV_HEADS):
            part = None
            for pp in range(ppb):
                kp = k_refs[nb * ppb + pp][pl.ds(g, page, stride=A_KV_HEADS), :]
                part = kp if part is None else part + kp
            ksum.append(jnp.sum(part, axis=0, keepdims=True))
        kmean = jnp.where(first_group, ksum[0], ksum[1]) * (1.0 / MOBA_BLOCK)
        gate = jnp.sum(q * kmean, axis=-1, keepdims=True)
        g_new = jnp.where(lane == c * bps + nb, gate, g_new)
    g_all[...] = g_new

    @pl.when(c == nc - 1)
    def _():
        g = jnp.where(lane < n_blocks, g_all[...], NEG)
        sel = jnp.zeros((A_HEADS, LANES), jnp.int32)
        for r in range(MOBA_TOPK):
            mx = jnp.max(g, axis=-1, keepdims=True)
            idx = jnp.min(jnp.where(g == mx, lane, LANES), axis=-1, keepdims=True)
            sel = jnp.where(lane == r, idx, sel)
            g = jnp.where(lane == idx, -3e38, g)
        sel_ref[...] = sel


def _moba_gate(page_table, ck, q, pages_per_step=32):
    nseq, n_pages = page_table.shape
    pool, rows, _ = ck.shape
    page = rows // A_KV_HEADS
    n_blocks = n_pages * page // MOBA_BLOCK
    pps = min(pages_per_step, n_pages)
    assert n_pages % pps == 0 and MOBA_TOPK <= n_blocks <= LANES
    per_seq = pl.BlockSpec((None, A_HEADS, A_HEAD_DIM), lambda b, c, pt: (b, 0, 0))
    grid_spec = pltpu.PrefetchScalarGridSpec(
        num_scalar_prefetch=1,
        grid=(nseq, n_pages // pps),
        in_specs=[pl.BlockSpec((None, rows, A_HEAD_DIM), lambda b, c, pt, p=p: (pt[b, c * pps + p], 0, 0))
                  for p in range(pps)] + [per_seq],
        out_specs=per_seq,
        scratch_shapes=[pltpu.VMEM((A_HEADS, LANES), F32)],
    )
    return pl.pallas_call(
        functools.partial(_moba_gate_kernel, pages_per_step=pps, n_blocks=n_blocks),
        grid_spec=grid_spec,
        out_shape=jax.ShapeDtypeStruct((nseq, A_HEADS, LANES), jnp.int32),
        compiler_params=_cparams(("parallel", "arbitrary")),
        name="moba_gate",
    )(page_table, *([ck] * pps), q)


def _moba_attend_kernel(pg_ref, blk_ref, *refs, n_sel_pages, n_blocks):
    del pg_ref
    nsp = n_sel_pages
    k_refs = refs[:nsp]
    v_refs = refs[nsp:2 * nsp]
    q_ref, knew_ref, vnew_ref, tlast_ref, bias0_ref, o_ref = refs[2 * nsp:]
    b = pl.program_id(0)
    h = pl.program_id(1)
    page = k_refs[0].shape[0] // A_KV_HEADS
    ppb = MOBA_BLOCK // page
    first_group = h < A_GROUP
    qs = q_ref[...] * (A_HEAD_DIM ** -0.5)

    def rows_of_head(ref):
        return jnp.where(first_group, ref[pl.ds(0, page, stride=A_KV_HEADS), :],
                         ref[pl.ds(1, page, stride=A_KV_HEADS), :])

    s_cols, v_pages = [], []
    for n in range(nsp):
        s = jnp.sum(rows_of_head(k_refs[n]) * qs, axis=-1, keepdims=True)
        blk = blk_ref[(b * A_HEADS + h) * MOBA_TOPK + n // ppb]
        bias = tlast_ref[pl.ds((n % ppb) * page, page), :]
        s_cols.append(s + jnp.where(blk == n_blocks - 1, bias, 0.0))
        v_pages.append(rows_of_head(v_refs[n]))
    s_new = jnp.sum(qs * knew_ref[...], axis=-1, keepdims=True) + bias0_ref[...]
    m = s_new
    for s in s_cols:
        m = jnp.maximum(m, jnp.max(s, axis=0, keepdims=True))
    p_new = jnp.exp(s_new - m)
    den = p_new
    acc = p_new * vnew_ref[...]
    for s, v in zip(s_cols, v_pages):
        p = jnp.exp(s - m)
        den = den + jnp.sum(p, axis=0, keepdims=True)
        acc = acc + jnp.sum(p * v, axis=0, keepdims=True)
    o_ref[...] = acc / den


def _moba_attend(sel_pages, sel_blocks, ck, cv, q, knew, vnew, tlast_col, bias0, n_blocks):
    nq = q.shape[0]
    nseq = nq // A_HEADS
    pool, rows, _ = ck.shape
    nsp = MOBA_TOPK * (MOBA_BLOCK // (rows // A_KV_HEADS))

    def page_spec(n):
        return pl.BlockSpec((None, rows, A_HEAD_DIM),
                            lambda b, h, pg, blk, n=n: (pg[(b * A_HEADS + h) * nsp + n], 0, 0))

    per_head = pl.BlockSpec((None, 1, A_HEAD_DIM), lambda b, h, pg, blk: (b * A_HEADS + h, 0, 0))
    grid_spec = pltpu.PrefetchScalarGridSpec(
        num_scalar_prefetch=2,
        grid=(nseq, A_HEADS),
        in_specs=[page_spec(n) for n in range(nsp)] + [page_spec(n) for n in range(nsp)]
                 + [per_head, per_head, per_head,
                    pl.BlockSpec((None, MOBA_BLOCK, 1), lambda b, h, pg, blk: (h, 0, 0)),
                    pl.BlockSpec((None, 1, 1), lambda b, h, pg, blk: (h, 0, 0))],
        out_specs=per_head,
    )
    return pl.pallas_call(
        functools.partial(_moba_attend_kernel, n_sel_pages=nsp, n_blocks=n_blocks),
        grid_spec=grid_spec,
        out_shape=jax.ShapeDtypeStruct((nq, 1, A_HEAD_DIM), F32),
        compiler_params=_cparams(("parallel", "parallel")),
        name="moba_attend",
    )(sel_pages, sel_blocks, *([ck] * nsp), *([cv] * nsp), q, knew, vnew, tlast_col, bias0)


def _mla_decode_kernel(pt_ref, *refs, pages_per_step, groups):
    del pt_ref
    pps = pages_per_step
    lat_refs = refs[:pps]
    kr_refs = refs[pps:2 * pps]
    ks_refs = refs[2 * pps:3 * pps]
    qabs_t_ref, qabs_ref, qpe_ref, latn_ref, kpen_ref, kscn_ref = refs[3 * pps:3 * pps + 6]
    o_ref = refs[3 * pps + 6]
    m_ref, l_ref, acc_ref = refs[3 * pps + 7:]
    c = pl.program_id(1)
    nc = pl.num_programs(1)
    scale = B_QK_DIM ** -0.5
    ppg = pps // groups

    @pl.when(c == 0)
    def _():
        m_ref[...] = jnp.full(m_ref.shape, NEG, F32)
        l_ref[...] = jnp.zeros(l_ref.shape, F32)
        acc_ref[...] = jnp.zeros(acc_ref.shape, F32)

    qabs_t = qabs_t_ref[...]
    qpe = qpe_ref[...]
    for g in range(groups):
        s_parts, lat_pairs = [], []
        for n in range(ppg // 2):
            i0 = g * ppg + 2 * n
            latb = jnp.concatenate([lat_refs[i0][...].astype(BF16), lat_refs[i0 + 1][...].astype(BF16)], axis=0)
            kpe_t = jnp.concatenate([kr_refs[i0][...], kr_refs[i0 + 1][...]], axis=1).astype(BF16)
            ksc_t = jnp.concatenate([ks_refs[i0][...], ks_refs[i0 + 1][...]], axis=1)
            s = _dot(latb, qabs_t).T[:B_HEADS] + _dot(qpe, kpe_t)
            s_parts.append(s * (ksc_t * scale))
            lat_pairs.append(latb)
        s = jnp.concatenate(s_parts, axis=1)
        m_old = m_ref[g]
        m_new = jnp.maximum(m_old, jnp.max(s, axis=-1, keepdims=True))
        alpha = jnp.exp(m_old - m_new)
        p = jnp.exp(s - m_new)
        l_ref[g] = alpha * l_ref[g] + jnp.sum(p, axis=-1, keepdims=True)
        pb = p.astype(BF16)
        acc = alpha * acc_ref[g]
        w = lat_pairs[0].shape[0]
        for n in range(ppg // 2):
            acc = acc + _dot(pb[:, n * w:(n + 1) * w], lat_pairs[n])
        acc_ref[g] = acc
        m_ref[g] = m_new

    @pl.when(c == nc - 1)
    def _():
        latn = latn_ref[...]
        s_new = (jnp.sum(qabs_ref[...].astype(F32) * latn.astype(BF16).astype(F32), axis=-1, keepdims=True)
                 + jnp.sum(qpe.astype(F32) * kpen_ref[...].astype(BF16).astype(F32), axis=-1, keepdims=True))
        s_new = s_new * (kscn_ref[...] * scale)
        m_f = s_new
        for g in range(groups):
            m_f = jnp.maximum(m_f, m_ref[g])
        p_new = jnp.exp(s_new - m_f)
        den = p_new
        num = p_new * latn
        for g in range(groups):
            a = jnp.exp(m_ref[g] - m_f)
            den = den + a * l_ref[g]
            num = num + a * acc_ref[g]
        o_ref[...] = num / den


def _mla_decode(page_table, c_lat, c_kr_t, c_ks_t, qabs_t, qabs, qpe, latn, kpen, kscn, pages_per_step=16, groups=2):
    nseq, n_pages = page_table.shape
    page = c_lat.shape[1]
    pps = pages_per_step
    assert n_pages % pps == 0 and pps % (2 * groups) == 0

    def page_spec(r, w):
        return [pl.BlockSpec((None, r, w), lambda b, c, pt, p=p: (pt[b, c * pps + p], 0, 0)) for p in range(pps)]

    seq3 = lambda r, w: pl.BlockSpec((None, r, w), lambda b, c, pt: (b, 0, 0))
    grid_spec = pltpu.PrefetchScalarGridSpec(
        num_scalar_prefetch=1,
        grid=(nseq, n_pages // pps),
        in_specs=page_spec(page, KV_LORA) + page_spec(ROPE_DIM, page) + page_spec(B_HEADS, page)
                 + [seq3(KV_LORA, LANES), seq3(B_HEADS, KV_LORA), seq3(B_HEADS, ROPE_DIM), seq3(1, KV_LORA),
                    seq3(1, ROPE_DIM), seq3(B_HEADS, 1)],
        out_specs=seq3(B_HEADS, KV_LORA),
        scratch_shapes=[pltpu.VMEM((groups, B_HEADS, 1), F32), pltpu.VMEM((groups, B_HEADS, 1), F32),
                        pltpu.VMEM((groups, B_HEADS, KV_LORA), F32)],
    )
    return pl.pallas_call(
        functools.partial(_mla_decode_kernel, pages_per_step=pps, groups=groups),
        grid_spec=grid_spec,
        out_shape=jax.ShapeDtypeStruct((nseq, B_HEADS, KV_LORA), F32),
        compiler_params=_cparams(("parallel", "arbitrary")),
        name="mla_decode",
    )(page_table, *([c_lat] * pps), *([c_kr_t] * pps), *([c_ks_t] * pps), qabs_t, qabs, qpe, latn, kpen, kscn)


def _head_nt_kernel(a_ref, w_ref, o_ref):
    o_ref[...] = _dot_nt(a_ref[...], w_ref[...])


def _absorb_q(qm_s, wuk):
    nseq = qm_s.shape[0]
    return pl.pallas_call(
        _head_nt_kernel,
        grid=(B_HEADS,),
        in_specs=[pl.BlockSpec((nseq, NOPE_DIM), lambda h: (0, 2 * h)),
                  pl.BlockSpec((KV_LORA, NOPE_DIM), lambda h: (0, h))],
        out_specs=pl.BlockSpec((None, nseq, KV_LORA), lambda h: (h, 0, 0)),
        out_shape=jax.ShapeDtypeStruct((B_HEADS, nseq, KV_LORA), F32),
        compiler_params=_cparams(("parallel",)),
        name="absorb_q",
    )(qm_s, wuk)


def _head_nn_kernel(a_ref, w_ref, o_ref):
    o_ref[...] = _dot(a_ref[...].astype(BF16), w_ref[...]).astype(o_ref.dtype)


def _expand_o(o_lat_h, wuv):
    _, nseq, _ = o_lat_h.shape
    return pl.pallas_call(
        _head_nn_kernel,
        grid=(B_HEADS,),
        in_specs=[pl.BlockSpec((None, nseq, KV_LORA), lambda h: (h, 0, 0)),
                  pl.BlockSpec((KV_LORA, V_DIM), lambda h: (0, h))],
        out_specs=pl.BlockSpec((nseq, V_DIM), lambda h: (0, h)),
        out_shape=jax.ShapeDtypeStruct((nseq, B_HEADS * V_DIM), BF16),
        compiler_params=_cparams(("parallel",)),
        name="expand_o",
    )(o_lat_h, wuv)


def _merge_kernel(oa_ref, ob_ref, ga_ref, gb_ref, wa_ref, wb_ref, o_ref):
    a = jax.nn.sigmoid(ga_ref[...]) * _dot(oa_ref[...], wa_ref[...])
    b = jax.nn.sigmoid(gb_ref[...]) * _dot(ob_ref[...], wb_ref[...])
    o_ref[...] = (a + b).astype(o_ref.dtype)


def _merge(oa, ob, proj, wa, wb, tm):
    n = oa.shape[0]
    d = wa.shape[1]
    return pl.pallas_call(
        _merge_kernel,
        grid=(n // tm,),
        in_specs=[pl.BlockSpec((tm, oa.shape[1]), lambda i: (i, 0)), pl.BlockSpec((tm, ob.shape[1]), lambda i: (i, 0)),
                  pl.BlockSpec((tm, d), lambda i: (i, COL_GA // d)), pl.BlockSpec((tm, d), lambda i: (i, COL_GB // d)),
                  pl.BlockSpec(wa.shape, lambda i: (0, 0)), pl.BlockSpec(wb.shape, lambda i: (0, 0))],
        out_specs=pl.BlockSpec((tm, d), lambda i: (i, 0)),
        out_shape=jax.ShapeDtypeStruct((n, d), BF16),
        compiler_params=_cparams(("parallel",)),
        name="merge",
    )(oa, ob, proj, proj, wa, wb)


def _outproj_router_kernel(t_ref, x_ref, wo_ref, g_ref, wr_ref, rb_ref, h_ref, hn_ref, ti_ref, tw_ref):
    h = x_ref[...] + _dot(t_ref[...], wo_ref[...])
    h_ref[...] = h
    inv = lax.rsqrt(jnp.mean(h * h, axis=-1, keepdims=True) + EPS)
    hn = h * inv * g_ref[...]
    hn_ref[...] = hn
    scores = jax.nn.sigmoid(_dot(hn.astype(BF16), wr_ref[...]))
    lane = _lane_iota(scores.shape)
    sel = jnp.where(lane < N_EXPERTS, scores + rb_ref[...], NEG)
    ti = jnp.zeros(scores.shape, jnp.int32)
    tw = jnp.zeros(scores.shape, F32)
    for r in range(TOP_K):
        mx = jnp.max(sel, axis=-1, keepdims=True)
        idx = jnp.min(jnp.where(sel == mx, lane, LANES), axis=-1, keepdims=True)
        pick = lane == idx
        w = jnp.sum(jnp.where(pick, scores, 0.0), axis=-1, keepdims=True)
        ti = jnp.where(lane == r, idx, ti)
        tw = jnp.where(lane == r, w, tw)
        sel = jnp.where(pick, -3e38, sel)
    tw = tw / jnp.sum(tw, axis=-1, keepdims=True) * ROUTED_SCALE
    ti_ref[...] = ti[:, :8]
    tw_ref[...] = tw[:, :8]


def _outproj_router(t, x, wo, g, wr, rb, tm):
    n, d = x.shape
    row = lambda w: pl.BlockSpec((tm, w), lambda i: (i, 0))
    const = lambda a: pl.BlockSpec(a.shape, lambda i: (0,) * a.ndim)
    return pl.pallas_call(
        _outproj_router_kernel,
        grid=(n // tm,),
        in_specs=[row(d), row(d), const(wo), const(g), const(wr), const(rb)],
        out_specs=[row(d), row(d), row(8), row(8)],
        out_shape=[jax.ShapeDtypeStruct((n, d), F32), jax.ShapeDtypeStruct((n, d), F32),
                   jax.ShapeDtypeStruct((n, 8), jnp.int32), jax.ShapeDtypeStruct((n, 8), F32)],
        compiler_params=_cparams(("parallel",)),
        name="outproj_router",
    )(t, x, wo, g, wr, rb)


def _experts_kernel(be_ref, nv_ref, x_ref, wg_ref, wu_ref, wd_ref, o_ref):
    del be_ref
    b = pl.program_id(0)

    @pl.when(b < nv_ref[0])
    def _():
        x = x_ref[...].astype(BF16)
        hid = jax.nn.silu(_dot(x, wg_ref[...].astype(BF16))) * _dot(x, wu_ref[...].astype(BF16))
        o_ref[...] = _dot(hid.astype(BF16), wd_ref[...].astype(BF16))

    @pl.when(b >= nv_ref[0])
    def _():
        o_ref[...] = jnp.zeros(o_ref.shape, o_ref.dtype)


def _experts(blk_expert, n_valid, xs, wg, wu, wd, tm):
    rows, d = xs.shape
    de = wg.shape[2]
    grid_spec = pltpu.PrefetchScalarGridSpec(
        num_scalar_prefetch=2,
        grid=(rows // tm,),
        in_specs=[pl.BlockSpec((tm, d), lambda b, be, nv: (b, 0)),
                  pl.BlockSpec((None, d, de), lambda b, be, nv: (be[b], 0, 0)),
                  pl.BlockSpec((None, d, de), lambda b, be, nv: (be[b], 0, 0)),
                  pl.BlockSpec((None, de, d), lambda b, be, nv: (be[b], 0, 0))],
        out_specs=pl.BlockSpec((tm, d), lambda b, be, nv: (b, 0)),
    )
    return pl.pallas_call(
        _experts_kernel,
        grid_spec=grid_spec,
        out_shape=jax.ShapeDtypeStruct((rows, d), F32),
        compiler_params=_cparams(("arbitrary",)),
        name="experts",
    )(blk_expert, n_valid, xs, wg, wu, wd)


def _shared_combine_kernel(hn_ref, h_ref, y_ref, tw_ref, wg_ref, wu_ref, wd_ref, o_ref):
    x = hn_ref[...].astype(BF16)
    hid = jax.nn.silu(_dot(x, wg_ref[...])) * _dot(x, wu_ref[...])
    acc = h_ref[...] + _dot(hid.astype(BF16), wd_ref[...])
    tw = tw_ref[...]
    for k in range(TOP_K):
        acc = acc + tw[:, k:k + 1] * y_ref[k]
    o_ref[...] = acc


def _shared_combine(hn, h, y_slots, tw, wg, wu, wd, tm):
    n, d = h.shape
    row = pl.BlockSpec((tm, d), lambda i: (i, 0))
    const = lambda a: pl.BlockSpec(a.shape, lambda i: (0, 0))
    return pl.pallas_call(
        _shared_combine_kernel,
        grid=(n // tm,),
        in_specs=[row, row, pl.BlockSpec((TOP_K, tm, d), lambda i: (0, i, 0)),
                  pl.BlockSpec((tm, tw.shape[1]), lambda i: (i, 0)), const(wg), const(wu), const(wd)],
        out_specs=row,
        out_shape=jax.ShapeDtypeStruct((n, d), F32),
        compiler_params=_cparams(("parallel",)),
        name="shared_combine",
    )(hn, h, y_slots, tw, wg, wu, wd)


def _rel_bucket(dist):
    dist = jnp.maximum(dist, 0)
    max_exact = REL_BUCKETS // 2
    d = jnp.maximum(dist, 1).astype(F32)
    large = max_exact + (jnp.log(d / max_exact) / math.log(REL_MAX_DIST / max_exact)
                         * (REL_BUCKETS - max_exact)).astype(jnp.int32)
    large = jnp.minimum(large, REL_BUCKETS - 1)
    return jnp.where(dist < max_exact, dist, large)


def _rope_table(pos):
    half = ROPE_DIM // 2
    freqs = jnp.power(ROPE_THETA, -jnp.arange(half, dtype=F32) / half)
    ang = pos.astype(F32)[:, None] * freqs
    cos, sin = jnp.cos(ang), jnp.sin(ang)
    return jnp.concatenate([cos, cos, -sin, sin], axis=-1)


def _prepare(w_in, a_q_norm, a_k_norm, b_q_a_norm, b_q_up, b_kv_a_norm, b_w_uk, b_w_uv, b_q_norm, b_k_norm):
    d = w_in.shape[0]
    half = ROPE_DIM // 2
    swap = lambda t: jnp.concatenate([t[..., half:], t[..., :half]], axis=-1)
    o = np.cumsum([0, A_Q_W, A_KV_W, A_KV_W, Q_LORA, KV_LORA, ROPE_DIM, d, d])
    aq, ak, av, bq, bkv, bkr, ga, gb = [w_in[:, o[i]:o[i + 1]] for i in range(8)]
    wcat = jnp.concatenate([ga, gb, aq, ak, av, bq, bkv, bkr, swap(bkr), jnp.zeros((d, LANES), F32)], axis=1)
    assert wcat.shape[1] == W_CAT_COLS
    q_pe_w = b_q_up[:, :, NOPE_DIM:]
    wq = jnp.concatenate([b_q_up[:, :, :NOPE_DIM], q_pe_w, swap(q_pe_w)], axis=-1)
    gq_head = jnp.concatenate([b_q_norm[:NOPE_DIM] * b_k_norm[:NOPE_DIM], b_q_norm[NOPE_DIM:], swap(b_q_norm[NOPE_DIM:])])
    return {
        "wcat": wcat.astype(BF16),
        "wq": wq.reshape(Q_LORA, B_HEADS * MLA_HEAD_PAD).astype(BF16),
        "wuk": b_w_uk.reshape(KV_LORA, B_HEADS * NOPE_DIM).astype(BF16),
        "wuv": b_w_uv.reshape(KV_LORA, B_HEADS * V_DIM).astype(BF16),
        "gaq": jnp.tile(a_q_norm, A_HEADS)[None, :],
        "gak": jnp.tile(a_k_norm, A_KV_HEADS)[None, :],
        "gbq": b_q_a_norm[None, :],
        "gbkv": b_kv_a_norm[None, :],
        "gq": jnp.tile(gq_head, B_HEADS)[None, :],
        "gk": jnp.concatenate([b_k_norm[NOPE_DIM:], swap(b_k_norm[NOPE_DIM:])])[None, :],
    }


def _toeplitz(w):
    hh, two_n = w.shape
    n = two_n // 2
    flat = jnp.tile(w, (1, n))[:, :n * (two_n - 1)]
    return flat.reshape(hh, n, two_n - 1)[:, :, :n]


def _bias_tables(rel_bias):
    dist = jnp.arange(2 * MOBA_BLOCK, dtype=jnp.int32)
    tbl = rel_bias[:, _rel_bucket(dist)] - rel_bias[:, REL_BUCKETS - 1:]
    n = MOBA_BLOCK
    hh = tbl.shape[0]
    rev = lambda t: t[:, ::-1]
    fill = jnp.zeros((hh, 1), F32)
    tdiag = _toeplitz(jnp.concatenate([tbl[:, :1], jnp.full((hh, n - 1), NEG, F32), fill, rev(tbl[:, 1:n])], axis=1))
    tprev = _toeplitz(jnp.concatenate([rev(tbl[:, 1:n + 1]), fill, rev(tbl[:, n + 1:])], axis=1))
    tlast = rev(tbl[:, 1:n + 1])
    return tdiag, tprev, tlast, tbl[:, :1]


def _dispatch(topi, tm):
    n = topi.shape[0]
    a = n * TOP_K
    e_flat = topi[:, :TOP_K].reshape(a)
    oh = (e_flat[:, None] == jnp.arange(N_EXPERTS, dtype=jnp.int32)[None, :]).astype(jnp.int32)
    csum = jnp.cumsum(oh, axis=0)
    counts = csum[-1]
    rank = jnp.sum(oh * csum, axis=1) - 1
    padded = (counts + tm - 1) // tm * tm
    pad_end = jnp.cumsum(padded)
    pad_start = pad_end - padded
    dest = jnp.sum(oh * pad_start[None, :], axis=1) + rank
    nb = -(-a // tm) + N_EXPERTS
    tok = jnp.arange(a, dtype=jnp.int32) // TOP_K
    row_tok = jnp.zeros((nb * tm,), jnp.int32).at[dest].set(tok)
    blk_expert = jnp.minimum(jnp.searchsorted(pad_end, jnp.arange(nb, dtype=jnp.int32) * tm, side="right"),
                             N_EXPERTS - 1).astype(jnp.int32)
    n_valid = (pad_end[-1:] // tm).astype(jnp.int32)
    return dest.reshape(n, TOP_K), row_tok, blk_expert, n_valid


def _tile(n, pref):
    t = pref
    while n % t:
        t //= 2
    return t


def kernel(x_prompt, x_sample, cache_moba_k, cache_moba_v, cache_mla_latent, cache_mla_krope, cache_mla_kscale,
           page_table, rel_bias, attn_norm, w_in, a_q_norm, a_k_norm, b_q_a_norm, b_q_up, b_kv_a_norm, b_w_uk,
           b_w_uv, b_q_norm, b_k_norm, w_branch_a, w_branch_b, w_out, ffn_norm, w_router, router_bias,
           w_e_gate, w_e_up, w_e_down, w_s_gate, w_s_up, w_s_down):
    depth = w_in.shape[0]
    assert depth == 1
    b_p, s_p, d = x_prompt.shape
    b_d, s_d, _ = x_sample.shape
    assert b_p == 1 and s_d == 1 and s_p % MOBA_BLOCK == 0
    pool, page = cache_moba_k.shape[1], cache_moba_k.shape[2]
    n_pages = page_table.shape[1]
    past = n_pages * page
    assert past % MOBA_BLOCK == 0 and MOBA_BLOCK % page == 0
    l = 0

    prm = _prepare(w_in[l], a_q_norm[l], a_k_norm[l], b_q_a_norm[l], b_q_up[l], b_kv_a_norm[l], b_w_uk[l],
                   b_w_uv[l], b_q_norm[l], b_k_norm[l])
    tdiag, tprev, tlast, bias0 = _bias_tables(rel_bias)
    g_attn = attn_norm[l][None, :]
    wa = w_branch_a[l].astype(BF16)
    wb = w_branch_b[l].astype(BF16)
    wo = w_out[l].astype(BF16)
    g_ffn = ffn_norm[l][None, :]
    wr = jnp.pad(w_router[l], ((0, 0), (0, LANES - N_EXPERTS))).astype(BF16)
    rb = jnp.pad(router_bias[l], (0, LANES - N_EXPERTS))[None, :]
    wsg, wsu, wsd = w_s_gate[l].astype(BF16), w_s_up[l].astype(BF16), w_s_down[l].astype(BF16)

    def token_front(x, pos, tm_front):
        n = x.shape[0]
        tm = _tile(n, 512)
        xn = _rmsnorm_bf16(x, g_attn, tm)
        proj = _matmul(xn, prm["wcat"], tm, W_CAT_COLS // 6)
        return proj, _front(proj, _rope_table(pos), prm, tm_front)

    def token_back(x, proj, oa, ob):
        n = x.shape[0]
        tm = _tile(n, 512)
        t = _merge(oa, ob, proj, wa, wb, tm)
        return _outproj_router(t, x, wo, g_ffn, wr, rb, _tile(n, 256))

    xp = x_prompt.reshape(s_p, d)
    proj_p, fp = token_front(xp, jnp.arange(s_p, dtype=jnp.int32), MOBA_BLOCK)
    qa, ka, va, kaug, vbf, kmean, qm, km, vm, lat, kpe, ksc = fp
    nblk = s_p // MOBA_BLOCK
    assert nblk <= LANES
    kmean_pad = jnp.pad(kmean.reshape(nblk, A_KV_HEADS, A_HEAD_DIM).transpose(1, 0, 2),
                        ((0, 0), (0, LANES - nblk), (0, 0)))
    tdiag_g = tdiag.reshape(A_KV_HEADS * A_GROUP, MOBA_BLOCK, MOBA_BLOCK)
    oa_p = _moba_prompt(qa, kaug, vbf, kmean_pad, tdiag_g, tprev)
    ob_p = _mla_prompt(qm, km, vm, _tile(s_p, 512), 2)
    h_p, hn_p, ti_p, tw_p = token_back(xp, proj_p, oa_p, ob_p)

    xd = x_sample.reshape(b_d, d)
    pos_d = jnp.full((b_d,), past, jnp.int32)
    proj_d, fd = token_front(xd, pos_d, _tile(b_d, 256))
    qa_d, ka_d, va_d, _, _, _, qm_d, _, _, lat_d, kpe_d, ksc_d = fd
    ck = cache_moba_k[l].reshape(pool, page * A_KV_HEADS, A_HEAD_DIM)
    cv = cache_moba_v[l].reshape(pool, page * A_KV_HEADS, A_HEAD_DIM)
    rep = lambda t: jnp.repeat(t.reshape(b_d, A_KV_HEADS, A_HEAD_DIM), A_GROUP, axis=1).reshape(-1, 1, A_HEAD_DIM)
    sel = _moba_gate(page_table, ck, qa_d.reshape(b_d, A_HEADS, A_HEAD_DIM))[:, :, :MOBA_TOPK]
    ppb = MOBA_BLOCK // page
    sel_pg = (sel[..., None] * ppb + jnp.arange(ppb, dtype=jnp.int32)).reshape(b_d, -1)
    sel_pages = jnp.take_along_axis(page_table, sel_pg, axis=1).reshape(-1)
    oa_d = _moba_attend(sel_pages, sel.reshape(-1), ck, cv, qa_d.reshape(-1, 1, A_HEAD_DIM), rep(ka_d), rep(va_d),
                        tlast[:, :, None], bias0[:, :, None], past // MOBA_BLOCK)
    oa_d = oa_d.reshape(b_d, A_Q_W).astype(BF16)

    qabs_h = _absorb_q(qm_d, prm["wuk"])
    qabs = qabs_h.transpose(1, 0, 2).astype(BF16)
    qabs_t = jnp.pad(qabs.transpose(0, 2, 1), ((0, 0), (0, 0), (0, LANES - B_HEADS)))
    qpe = qm_d.reshape(b_d, B_HEADS, MLA_HEAD_PAD)[:, :, NOPE_DIM:NOPE_DIM + ROPE_DIM]
    c_kr_t = jnp.swapaxes(cache_mla_krope[l], 1, 2)
    c_ks_t = jnp.swapaxes(cache_mla_kscale[l], 1, 2)
    o_lat = _mla_decode(page_table, cache_mla_latent[l], c_kr_t, c_ks_t, qabs_t, qabs, qpe,
                        lat_d[:, None, :], kpe_d[:, None, :], ksc_d[:, :, None])
    ob_d = _expand_o(o_lat.transpose(1, 0, 2), prm["wuv"])
    h_d, hn_d, ti_d, tw_d = token_back(xd, proj_d, oa_d, ob_d)

    n_all = s_p + b_d
    hn_all = jnp.concatenate([hn_p, hn_d], axis=0)
    ti_all = jnp.concatenate([ti_p, ti_d], axis=0)
    tw_all = jnp.concatenate([tw_p, tw_d], axis=0)
    del n_all, tw_all
    tm_e = 256
    dest, row_tok, blk_expert, n_valid = _dispatch(ti_all, tm_e)
    xs = hn_all[row_tok]
    yb = _experts(blk_expert, n_valid, xs, w_e_gate[l], w_e_up[l], w_e_down[l], tm_e)
    y_p = _shared_combine(hn_p, h_p, yb[dest[:s_p].T], tw_p, wsg, wsu, wsd, _tile(s_p, 128))
    y_d = _shared_combine(hn_d, h_d, yb[dest[s_p:].T], tw_d, wsg, wsu, wsd, _tile(b_d, 128))

    y_prompt = y_p.reshape(b_p, s_p, d)
    y_sample = y_d.reshape(b_d, s_d, d)
    return (y_prompt, y_sample,
            ka.reshape(1, b_p, s_p, A_KV_HEADS, A_HEAD_DIM), va.reshape(1, b_p, s_p, A_KV_HEADS, A_HEAD_DIM),
            lat.reshape(1, b_p, s_p, KV_LORA), kpe.reshape(1, b_p, s_p, ROPE_DIM), ksc.reshape(1, b_p, s_p, B_HEADS),
            ka_d.reshape(1, b_d, s_d, A_KV_HEADS, A_HEAD_DIM), va_d.reshape(1, b_d, s_d, A_KV_HEADS, A_HEAD_DIM),
            lat_d.reshape(1, b_d, s_d, KV_LORA), kpe_d.reshape(1, b_d, s_d, ROPE_DIM),
            ksc_d.reshape(1, b_d, s_d, B_HEADS))
```

```python
import functools
import math

import numpy as np
import jax
import jax.numpy as jnp
from jax import lax
from jax.experimental import pallas as pl
from jax.experimental.pallas import tpu as pltpu

F32 = jnp.float32
BF16 = jnp.bfloat16

A_HEADS = 8
A_KV_HEADS = 2
A_HEAD_DIM = 128
A_GROUP = A_HEADS // A_KV_HEADS
MOBA_BLOCK = 256
MOBA_TOPK = 3
MOBA_FAR_GROUP = 4
B_HEADS = 8
Q_LORA = 512
KV_LORA = 512
NOPE_DIM = 128
ROPE_DIM = 64
V_DIM = 128
B_QK_DIM = NOPE_DIM + ROPE_DIM
ROPE_THETA = 10000.0
REL_BUCKETS = 32
REL_MAX_DIST = 128
N_EXPERTS = 64
TOP_K = 6
ROUTED_SCALE = 1.0
EPS = 1e-6

LANES = 128
MLA_HEAD_PAD = 256
NEG = -1e30
VMEM_LIMIT = 56 * 1024 * 1024

A_Q_W = A_HEADS * A_HEAD_DIM
A_KV_W = A_KV_HEADS * A_HEAD_DIM

COL_GA = 0
COL_GB = 2048
COL_AQ = 4096
COL_AK = 5120
COL_AV = 5376
COL_BQ = 5632
COL_BKV = 6144
COL_BKR = 6656
W_CAT_COLS = 6912


def _cparams(sem):
    return pltpu.CompilerParams(dimension_semantics=sem, vmem_limit_bytes=VMEM_LIMIT)


def _dot(a, b):
    return jnp.dot(a, b, preferred_element_type=F32)


def _dot_nt(a, b):
    return lax.dot_general(a, b, (((1,), (1,)), ((), ())), preferred_element_type=F32)


def _split_bf16(a):
    hi = a.astype(BF16)
    lo = (a - hi.astype(F32)).astype(BF16)
    return hi, lo


def _dot3(a, b):
    ah, al = _split_bf16(a)
    bh, bl = _split_bf16(b)
    return _dot(ah, bh) + (_dot(ah, bl) + _dot(al, bh))


def _lane_iota(shape):
    return lax.broadcasted_iota(jnp.int32, shape, len(shape) - 1)


def _rmsnorm_kernel(x_ref, g_ref, o_ref):
    x = x_ref[...]
    inv = lax.rsqrt(jnp.mean(x * x, axis=-1, keepdims=True) + EPS)
    o_ref[...] = (x * inv * g_ref[...]).astype(o_ref.dtype)


def _rmsnorm_bf16(x, g, tm):
    n, d = x.shape
    return pl.pallas_call(
        _rmsnorm_kernel,
        grid=(n // tm,),
        in_specs=[pl.BlockSpec((tm, d), lambda i: (i, 0)), pl.BlockSpec((1, d), lambda i: (0, 0))],
        out_specs=pl.BlockSpec((tm, d), lambda i: (i, 0)),
        out_shape=jax.ShapeDtypeStruct((n, d), BF16),
        compiler_params=_cparams(("parallel",)),
        name="rmsnorm",
    )(x, g)


def _matmul_kernel(a_ref, b_ref, o_ref):
    o_ref[...] = _dot(a_ref[...], b_ref[...])


def _matmul(a, b, tm, tn):
    m, k = a.shape
    _, n = b.shape
    return pl.pallas_call(
        _matmul_kernel,
        grid=(n // tn, m // tm),
        in_specs=[pl.BlockSpec((tm, k), lambda j, i: (i, 0)), pl.BlockSpec((k, tn), lambda j, i: (0, j))],
        out_specs=pl.BlockSpec((tm, tn), lambda j, i: (i, j)),
        out_shape=jax.ShapeDtypeStruct((m, n), F32),
        compiler_params=_cparams(("parallel", "parallel")),
        name="in_proj",
    )(a, b)


def _front_kernel(aq_ref, ak_ref, av_ref, bq_ref, bkv_ref, bkr_ref, cs_ref,
                  gaq_ref, gak_ref, gbq_ref, gbkv_ref, gq_ref, gk_ref,
                  wq_ref, wuk_ref, wuv_ref,
                  qa_ref, ka_ref, va_ref, kaug_ref, vbf_ref, kmean_ref,
                  qm_ref, km_ref, vm_ref, lat_ref, kpe_ref, ksc_ref, *, tm):
    pid = pl.program_id(0)
    lane = _lane_iota((tm, LANES))
    lo = lane < ROPE_DIM
    cs = cs_ref[...]

    def head_norm(x, g):
        inv = lax.rsqrt(jnp.mean(x * x, axis=-1, keepdims=True) + EPS)
        return x * inv * g

    for h in range(A_HEADS):
        sl = slice(h * A_HEAD_DIM, (h + 1) * A_HEAD_DIM)
        qa_ref[:, sl] = head_norm(aq_ref[:, sl], gaq_ref[:, sl])
    row = lax.broadcasted_iota(jnp.int32, (tm, LANES), 0) + pid * tm
    onehot = jnp.where(lane == row // MOBA_BLOCK, 1.0, 0.0).astype(BF16)
    for g in range(A_KV_HEADS):
        sl = slice(g * A_HEAD_DIM, (g + 1) * A_HEAD_DIM)
        k = head_norm(ak_ref[:, sl], gak_ref[:, sl])
        ka_ref[:, sl] = k
        kmean_ref[0, :, sl] = jnp.mean(k, axis=0, keepdims=True)
        kaug_ref[:, 2 * g * LANES:(2 * g + 1) * LANES] = k.astype(BF16)
        kaug_ref[:, (2 * g + 1) * LANES:(2 * g + 2) * LANES] = onehot
    av = av_ref[...]
    va_ref[...] = av
    ones = jnp.ones((tm, LANES), BF16)
    for g in range(A_KV_HEADS):
        vbf_ref[:, 2 * g * LANES:(2 * g + 1) * LANES] = av[:, g * A_HEAD_DIM:(g + 1) * A_HEAD_DIM].astype(BF16)
        vbf_ref[:, (2 * g + 1) * LANES:(2 * g + 2) * LANES] = ones

    qc = head_norm(bq_ref[...], gbq_ref[...]).astype(BF16)
    qraw = _dot(qc, wq_ref[...])
    for h in range(B_HEADS):
        a = qraw[:, h * MLA_HEAD_PAD:h * MLA_HEAD_PAD + LANES]
        b = qraw[:, h * MLA_HEAD_PAD + LANES:(h + 1) * MLA_HEAD_PAD]
        ssq = jnp.sum(a * a, axis=-1, keepdims=True) + jnp.sum(jnp.where(lo, b * b, 0.0), axis=-1, keepdims=True)
        inv = lax.rsqrt(ssq / B_QK_DIM + EPS)
        qm_ref[:, h * MLA_HEAD_PAD:h * MLA_HEAD_PAD + LANES] = (
            a * inv * gq_ref[:, h * MLA_HEAD_PAD:h * MLA_HEAD_PAD + LANES]).astype(BF16)
        t = b * inv * gq_ref[:, h * MLA_HEAD_PAD + LANES:(h + 1) * MLA_HEAD_PAD] * cs
        pe = jnp.where(lo, t + pltpu.roll(t, ROPE_DIM, axis=1), 0.0)
        qm_ref[:, h * MLA_HEAD_PAD + LANES:(h + 1) * MLA_HEAD_PAD] = pe.astype(BF16)

    lat = head_norm(bkv_ref[...], gbkv_ref[...])
    lat_ref[...] = lat
    latb = lat.astype(BF16)
    knope = _dot(latb, wuk_ref[...])
    vb = _dot(latb, wuv_ref[...]).astype(BF16)
    for h in range(B_HEADS):
        vm_ref[:, 2 * h * LANES:(2 * h + 1) * LANES] = vb[:, h * V_DIM:(h + 1) * V_DIM]
        vm_ref[:, (2 * h + 1) * LANES:(2 * h + 2) * LANES] = ones
    bkr = bkr_ref[...]
    rss = jnp.sum(jnp.where(lo, bkr * bkr, 0.0), axis=-1, keepdims=True)
    t = bkr * gk_ref[...] * cs
    kpe = jnp.where(lo, t + pltpu.roll(t, ROPE_DIM, axis=1), 0.0)
    kpe_ref[...] = kpe[:, :ROPE_DIM]
    ksw = jnp.zeros((tm, LANES), F32)
    scale = B_QK_DIM ** -0.5
    for h in range(B_HEADS):
        kh = knope[:, h * NOPE_DIM:(h + 1) * NOPE_DIM]
        ssq = jnp.sum(kh * kh, axis=-1, keepdims=True) + rss
        ks = lax.rsqrt(ssq / B_QK_DIM + EPS)
        ksw = jnp.where(lane == h, ks, ksw)
        km_ref[:, h * MLA_HEAD_PAD:h * MLA_HEAD_PAD + LANES] = (kh * (ks * scale)).astype(BF16)
        km_ref[:, h * MLA_HEAD_PAD + LANES:(h + 1) * MLA_HEAD_PAD] = (kpe * (ks * scale)).astype(BF16)
    ksc_ref[...] = ksw[:, :B_HEADS]


def _front(proj, cs, prm, tm):
    n = proj.shape[0]
    nt = n // tm
    row = lambda w, c: pl.BlockSpec((tm, w), lambda i, c=c, w=w: (i, c // w))
    const = lambda a: pl.BlockSpec(a.shape, lambda i: (0,) * a.ndim)
    consts = [prm["gaq"], prm["gak"], prm["gbq"], prm["gbkv"], prm["gq"], prm["gk"], prm["wq"], prm["wuk"], prm["wuv"]]
    out_shapes = [
        ((n, A_Q_W), F32), ((n, A_KV_W), F32), ((n, A_KV_W), F32),
        ((n, 2 * A_KV_W), BF16), ((n, 2 * A_KV_W), BF16),
        ((nt, 1, A_KV_W), F32),
        ((n, B_HEADS * MLA_HEAD_PAD), BF16), ((n, B_HEADS * MLA_HEAD_PAD), BF16),
        ((n, B_HEADS * 2 * V_DIM), BF16),
        ((n, KV_LORA), F32), ((n, ROPE_DIM), F32), ((n, B_HEADS), F32),
    ]
    out_specs = []
    for shp, _ in out_shapes:
        if len(shp) == 3:
            out_specs.append(pl.BlockSpec((1, 1, shp[2]), lambda i: (i, 0, 0)))
        else:
            out_specs.append(pl.BlockSpec((tm, shp[1]), lambda i: (i, 0)))
    return pl.pallas_call(
        functools.partial(_front_kernel, tm=tm),
        grid=(nt,),
        in_specs=[row(A_Q_W, COL_AQ), row(A_KV_W, COL_AK), row(A_KV_W, COL_AV), row(Q_LORA, COL_BQ),
                  row(KV_LORA, COL_BKV), row(LANES, COL_BKR), pl.BlockSpec((tm, LANES), lambda i: (i, 0))]
                 + [const(a) for a in consts],
        out_specs=out_specs,
        out_shape=[jax.ShapeDtypeStruct(s, d) for s, d in out_shapes],
        compiler_params=_cparams(("parallel",)),
        name="front",
    )(proj, proj, proj, proj, proj, proj, cs, *consts)


def _top_blocks(gate, valid, lane):
    g = jnp.where(valid, gate, NEG)
    lane_f = lane.astype(F32)
    selv = jnp.zeros(gate.shape, F32)
    for _ in range(MOBA_TOPK):
        mx = jnp.max(g, axis=-1, keepdims=True)
        idx = jnp.min(jnp.where(g == mx, lane_f, float(LANES)), axis=-1, keepdims=True)
        pick = lane_f == idx
        selv = jnp.where(pick, 1.0, selv)
        g = jnp.where(pick, -3e38, g)
    return jnp.where(valid, selv, 0.0)


def _rep(x, width):
    return x if width == LANES else jnp.concatenate([x] * (width // LANES), axis=1)


def _flash_first(s, v, m_ref, acc_ref, idx):
    m = jnp.max(s, axis=-1, keepdims=True)
    p = jnp.exp(s - m)
    m_ref[idx] = jnp.broadcast_to(m, (s.shape[0], LANES))
    acc_ref[idx] = _dot(p.astype(BF16), v)


def _flash_next(s, v, m_ref, acc_ref, idx):
    m_old = m_ref[idx]
    m_new = jnp.maximum(m_old, jnp.max(s, axis=-1, keepdims=True))
    alpha = jnp.exp(m_old - m_new)
    p = jnp.exp(s - _rep(m_new, s.shape[-1]))
    acc_ref[idx] = _rep(alpha, 2 * LANES) * acc_ref[idx] + _dot(p.astype(BF16), v)
    m_ref[idx] = m_new


def _flash_out(acc):
    return acc[:, :LANES] / acc[:, LANES:]


def _moba_prompt_kernel(q_ref, kaug_ref, v_ref, kmean_ref, tdiag_ref, tprev_ref, o_ref,
                        qaug_ref, qfar_ref, m_ref, acc_ref):
    i = pl.program_id(1)
    blk = MOBA_BLOCK
    lane = _lane_iota((blk, LANES))
    valid = lane < i
    kmean = kmean_ref[...]
    scale = A_HEAD_DIM ** -0.5

    diag0 = pl.multiple_of(i * blk, blk)
    kd = kaug_ref[pl.ds(diag0, blk), :]
    vd = v_ref[pl.ds(diag0, blk), :]
    for hh in range(A_GROUP):
        q = q_ref[:, hh * A_HEAD_DIM:(hh + 1) * A_HEAD_DIM]
        qh, ql = _split_bf16(q)
        kh, kl = _split_bf16(kmean)
        gate = _dot_nt(qh, kh) + (_dot_nt(qh, kl) + _dot_nt(ql, kh))
        selv = _top_blocks(gate, valid, lane)
        on = jnp.where(lane == i, 1.0, selv)
        qs = (q * scale).astype(BF16)
        qaug_ref[hh, :, :LANES] = qs
        qaug_ref[hh, :, LANES:] = jnp.where(on > 0.0, 0.0, NEG).astype(BF16)
        qfar_ref[hh, :, :LANES] = qs
        qfar_ref[hh, :, LANES:] = jnp.where(lane < i - 1, jnp.where(selv > 0.0, 0.0, NEG), NEG).astype(BF16)
        _flash_first(_dot_nt(qaug_ref[hh], kd) + tdiag_ref[hh], vd, m_ref, acc_ref, hh)

    @pl.when(i >= 1)
    def _():
        j0 = pl.multiple_of((i - 1) * blk, blk)
        kj = kaug_ref[pl.ds(j0, blk), :]
        vj = v_ref[pl.ds(j0, blk), :]
        for hh in range(A_GROUP):
            _flash_next(_dot_nt(qaug_ref[hh], kj) + tprev_ref[hh], vj, m_ref, acc_ref, hh)

    span = MOBA_FAR_GROUP * blk

    def far(g, carry):
        j0 = pl.multiple_of(g * span, span)
        kj = kaug_ref[pl.ds(j0, span), :]
        vj = v_ref[pl.ds(j0, span), :]
        for hh in range(A_GROUP):
            _flash_next(_dot_nt(qfar_ref[hh], kj), vj, m_ref, acc_ref, hh)
        return carry

    lax.fori_loop(0, (jnp.maximum(i - 1, 0) + MOBA_FAR_GROUP - 1) // MOBA_FAR_GROUP, far, 0)

    for hh in range(A_GROUP):
        o_ref[:, hh * A_HEAD_DIM:(hh + 1) * A_HEAD_DIM] = _flash_out(acc_ref[hh]).astype(o_ref.dtype)


def _moba_prompt(qa, kaug, vbf, kmean_pad, tdiag, tprev):
    s = qa.shape[0]
    nblk = s // MOBA_BLOCK
    assert nblk % MOBA_FAR_GROUP == 0
    gw = A_GROUP * A_HEAD_DIM
    return pl.pallas_call(
        _moba_prompt_kernel,
        grid=(A_KV_HEADS, nblk),
        in_specs=[
            pl.BlockSpec((MOBA_BLOCK, gw), lambda g, i: (i, g)),
            pl.BlockSpec((s, 2 * LANES), lambda g, i: (0, g)),
            pl.BlockSpec((s, 2 * LANES), lambda g, i: (0, g)),
            pl.BlockSpec((None, LANES, A_HEAD_DIM), lambda g, i: (g, 0, 0)),
            pl.BlockSpec((A_GROUP, MOBA_BLOCK, MOBA_BLOCK), lambda g, i: (g, 0, 0)),
            pl.BlockSpec((A_GROUP, MOBA_BLOCK, MOBA_BLOCK), lambda g, i: (g, 0, 0)),
        ],
        out_specs=pl.BlockSpec((MOBA_BLOCK, gw), lambda g, i: (i, g)),
        out_shape=jax.ShapeDtypeStruct((s, A_Q_W), BF16),
        scratch_shapes=[
            pltpu.VMEM((A_GROUP, MOBA_BLOCK, 2 * LANES), BF16),
            pltpu.VMEM((A_GROUP, MOBA_BLOCK, 2 * LANES), BF16),
            pltpu.VMEM((A_GROUP, MOBA_BLOCK, LANES), F32),
            pltpu.VMEM((A_GROUP, MOBA_BLOCK, 2 * LANES), F32),
        ],
        compiler_params=_cparams(("parallel", "parallel")),
        name="moba_prompt",
    )(qa, kaug, vbf, kmean_pad, tdiag, tprev)


def _mla_prompt_kernel(q_ref, k_ref, v_ref, o_ref, m_ref, acc_ref, *, tq, parts):
    qi = pl.program_id(1)
    rows = tq // parts
    d0 = pl.multiple_of(qi * tq, tq)
    kd = k_ref[pl.ds(d0, tq), :]
    vd = v_ref[pl.ds(d0, tq), :]
    c = lax.broadcasted_iota(jnp.int32, (rows, tq), 1)
    for a in range(parts):
        r = lax.broadcasted_iota(jnp.int32, (rows, tq), 0) + a * rows
        s = jnp.where(c <= r, _dot_nt(q_ref[a], kd), NEG)
        _flash_first(s, vd, m_ref, acc_ref, a)

    def body(j, carry):
        j0 = pl.multiple_of(j * tq, tq)
        kj = k_ref[pl.ds(j0, tq), :]
        vj = v_ref[pl.ds(j0, tq), :]
        for a in range(parts):
            _flash_next(_dot_nt(q_ref[a], kj), vj, m_ref, acc_ref, a)
        return carry

    lax.fori_loop(0, qi, body, 0)
    for a in range(parts):
        o_ref[a] = _flash_out(acc_ref[a]).astype(o_ref.dtype)


def _mla_prompt(qm, km, vm, tq, parts):
    s = qm.shape[0]
    rows = tq // parts
    out = pl.pallas_call(
        functools.partial(_mla_prompt_kernel, tq=tq, parts=parts),
        grid=(B_HEADS, s // tq),
        in_specs=[
            pl.BlockSpec((parts, rows, MLA_HEAD_PAD), lambda h, i: (i, 0, h)),
            pl.BlockSpec((s, MLA_HEAD_PAD), lambda h, i: (0, h)),
            pl.BlockSpec((s, 2 * V_DIM), lambda h, i: (0, h)),
        ],
        out_specs=pl.BlockSpec((parts, rows, V_DIM), lambda h, i: (i, 0, h)),
        out_shape=jax.ShapeDtypeStruct((s // rows, rows, B_HEADS * V_DIM), BF16),
        scratch_shapes=[pltpu.VMEM((parts, rows, LANES), F32), pltpu.VMEM((parts, rows, 2 * LANES), F32)],
        compiler_params=_cparams(("parallel", "parallel")),
        name="mla_prompt",
    )(qm.reshape(s // rows, rows, B_HEADS * MLA_HEAD_PAD), km, vm)
    return out.reshape(s, B_HEADS * V_DIM)


def _moba_gate_kernel(pt_ref, *refs, pages_per_step, n_blocks):
    del pt_ref
    pps = pages_per_step
    k_refs = refs[:pps]
    q_ref, sel_ref, g_all = refs[pps:]
    c = pl.program_id(1)
    nc = pl.num_programs(1)
    page = k_refs[0].shape[0] // A_KV_HEADS
    ppb = MOBA_BLOCK // page
    bps = pps // ppb
    lane = _lane_iota((A_HEADS, LANES))
    head = lax.broadcasted_iota(jnp.int32, (A_HEADS, LANES), 0)
    first_group = head < A_GROUP
    q = q_ref[...]

    @pl.when(c == 0)
    def _():
        g_all[...] = jnp.full((A_HEADS, LANES), NEG, F32)

    g_new = g_all[...]
    for nb in range(bps):
        ksum = []
        for g in range(A_KV_HEADS):
            part = None
            for pp in range(ppb):
                kp = k_refs[nb * ppb + pp][pl.ds(g, page, stride=A_KV_HEADS), :]
                part = kp if part is None else part + kp
            ksum.append(jnp.sum(part, axis=0, keepdims=True))
        kmean = jnp.where(first_group, ksum[0], ksum[1]) * (1.0 / MOBA_BLOCK)
        gate = jnp.sum(q * kmean, axis=-1, keepdims=True)
        g_new = jnp.where(lane == c * bps + nb, gate, g_new)
    g_all[...] = g_new

    @pl.when(c == nc - 1)
    def _():
        g = jnp.where(lane < n_blocks, g_all[...], NEG)
        lane_f = lane.astype(F32)
        sel = jnp.zeros((A_HEADS, LANES), F32)
        for r in range(MOBA_TOPK):
            mx = jnp.max(g, axis=-1, keepdims=True)
            idx = jnp.min(jnp.where(g == mx, lane_f, float(LANES)), axis=-1, keepdims=True)
            sel = jnp.where(lane == r, idx, sel)
            g = jnp.where(lane_f == idx, -3e38, g)
        sel_ref[...] = sel.astype(jnp.int32)


def _moba_gate(page_table, ck, q, pages_per_step=32):
    nseq, n_pages = page_table.shape
    pool, rows, _ = ck.shape
    page = rows // A_KV_HEADS
    n_blocks = n_pages * page // MOBA_BLOCK
    pps = min(pages_per_step, n_pages)
    assert n_pages % pps == 0 and MOBA_TOPK <= n_blocks <= LANES
    per_seq = pl.BlockSpec((None, A_HEADS, A_HEAD_DIM), lambda b, c, pt: (b, 0, 0))
    grid_spec = pltpu.PrefetchScalarGridSpec(
        num_scalar_prefetch=1,
        grid=(nseq, n_pages // pps),
        in_specs=[pl.BlockSpec((None, rows, A_HEAD_DIM), lambda b, c, pt, p=p: (pt[b, c * pps + p], 0, 0))
                  for p in range(pps)] + [per_seq],
        out_specs=per_seq,
        scratch_shapes=[pltpu.VMEM((A_HEADS, LANES), F32)],
    )
    return pl.pallas_call(
        functools.partial(_moba_gate_kernel, pages_per_step=pps, n_blocks=n_blocks),
        grid_spec=grid_spec,
        out_shape=jax.ShapeDtypeStruct((nseq, A_HEADS, LANES), jnp.int32),
        compiler_params=_cparams(("parallel", "arbitrary")),
        name="moba_gate",
    )(page_table, *([ck] * pps), q)


def _moba_attend_kernel(pg_ref, blk_ref, k_hbm, v_hbm, q_ref, knew_ref, vnew_ref, tlast_ref, bias0_ref, o_ref,
                        kbuf, vbuf, sem, *, n_sel_pages, n_blocks, page):
    nsp = n_sel_pages
    per_seq = A_HEADS * nsp
    t = pl.program_id(0)
    nt = pl.num_programs(0)
    ppb = MOBA_BLOCK // page
    scale = A_HEAD_DIM ** -0.5

    def seq_copies(seq, slot):
        out = []
        for i in range(per_seq):
            pid = pg_ref[seq * per_seq + i]
            out.append(pltpu.make_async_copy(k_hbm.at[pid], kbuf.at[slot, i], sem.at[slot, 0]))
            out.append(pltpu.make_async_copy(v_hbm.at[pid], vbuf.at[slot, i], sem.at[slot, 1]))
        return out

    def start(seq, slot):
        for cp in seq_copies(seq, slot):
            cp.start()

    def wait(slot):
        for cp in seq_copies(0, slot):
            cp.wait()

    def attend(seq, slot):
        for h in range(A_HEADS):
            g = h // A_GROUP
            qs = q_ref[slot, h:h + 1, :] * scale
            s_cols, v_pages = [], []
            for n in range(nsp):
                k = kbuf[slot, h * nsp + n, pl.ds(g, page, stride=A_KV_HEADS), :]
                s = jnp.sum(k * qs, axis=-1, keepdims=True)
                blk = blk_ref[(seq * A_HEADS + h) * MOBA_TOPK + n // ppb]
                bias = tlast_ref[h, pl.ds((n % ppb) * page, page), :]
                s_cols.append(s + jnp.where(blk == n_blocks - 1, bias, 0.0))
                v_pages.append(vbuf[slot, h * nsp + n, pl.ds(g, page, stride=A_KV_HEADS), :])
            s_new = jnp.sum(qs * knew_ref[slot, h:h + 1, :], axis=-1, keepdims=True) + bias0_ref[h]
            m = s_new
            for s in s_cols:
                m = jnp.maximum(m, jnp.max(s, axis=0, keepdims=True))
            p_new = jnp.exp(s_new - m)
            den = p_new
            acc = p_new * vnew_ref[slot, h:h + 1, :]
            for s, v in zip(s_cols, v_pages):
                p = jnp.exp(s - m)
                den = den + jnp.sum(p, axis=0, keepdims=True)
                acc = acc + jnp.sum(p * v, axis=0, keepdims=True)
            o_ref[slot, h:h + 1, :] = acc / den

    @pl.when(t == 0)
    def _():
        start(0, 0)

    wait(0)
    start(2 * t + 1, 1)
    attend(2 * t, 0)
    wait(1)
    start(jnp.minimum(2 * t + 2, 2 * nt - 1), 0)
    attend(2 * t + 1, 1)

    @pl.when(t == nt - 1)
    def _():
        wait(0)


def _moba_attend(sel_pages, sel_blocks, ck, cv, q, knew, vnew, tlast_col, bias0, n_blocks):
    nseq = q.shape[0]
    assert nseq % 2 == 0
    pool, rows, _ = ck.shape
    page = rows // A_KV_HEADS
    nsp = MOBA_TOPK * (MOBA_BLOCK // page)
    hbm = pl.BlockSpec(memory_space=pl.ANY)
    pair = pl.BlockSpec((2, A_HEADS, A_HEAD_DIM), lambda t, pg, blk: (t, 0, 0))
    full = lambda a: pl.BlockSpec(a.shape, lambda t, pg, blk: (0,) * a.ndim)
    grid_spec = pltpu.PrefetchScalarGridSpec(
        num_scalar_prefetch=2,
        grid=(nseq // 2,),
        in_specs=[hbm, hbm, pair, pair, pair, full(tlast_col), full(bias0)],
        out_specs=pair,
        scratch_shapes=[pltpu.VMEM((2, A_HEADS * nsp, rows, A_HEAD_DIM), F32),
                        pltpu.VMEM((2, A_HEADS * nsp, rows, A_HEAD_DIM), F32),
                        pltpu.SemaphoreType.DMA((2, 2))],
    )
    return pl.pallas_call(
        functools.partial(_moba_attend_kernel, n_sel_pages=nsp, n_blocks=n_blocks, page=page),
        grid_spec=grid_spec,
        out_shape=jax.ShapeDtypeStruct((nseq, A_HEADS, A_HEAD_DIM), F32),
        compiler_params=_cparams(("arbitrary",)),
        name="moba_attend",
    )(sel_pages, sel_blocks, ck, cv, q, knew, vnew, tlast_col, bias0)


def _mla_decode_kernel(pt_ref, lat_hbm, kr_hbm, ks_hbm, qabs_t_ref, qabs_ref, qpe_ref, latn_ref, kpen_ref, kscn_ref,
                       o_ref, lat_buf, kr_buf, ks_buf, sem, m_ref, l_ref, acc_ref, *, pages_per_step, groups, n_chunks):
    pps = pages_per_step
    b = pl.program_id(0)
    nb = pl.num_programs(0)
    scale = B_QK_DIM ** -0.5
    ppg = pps // groups

    def chunk_copies(seq, chunk, slot):
        out = []
        for p in range(pps):
            pid = pt_ref[seq, chunk * pps + p]
            out.append(pltpu.make_async_copy(lat_hbm.at[pid], lat_buf.at[slot, p], sem.at[slot, 0]))
            out.append(pltpu.make_async_copy(kr_hbm.at[pid], kr_buf.at[slot, p], sem.at[slot, 1]))
            out.append(pltpu.make_async_copy(ks_hbm.at[pid], ks_buf.at[slot, p], sem.at[slot, 2]))
        return out

    def start(seq, chunk, slot):
        for cp in chunk_copies(seq, chunk, slot):
            cp.start()

    def wait(slot):
        for cp in chunk_copies(0, 0, slot):
            cp.wait()

    @pl.when(b == 0)
    def _():
        start(0, 0, 0)

    m_ref[...] = jnp.full(m_ref.shape, NEG, F32)
    l_ref[...] = jnp.zeros(l_ref.shape, F32)
    acc_ref[...] = jnp.zeros(acc_ref.shape, F32)
    qabs_t = qabs_t_ref[...]
    qpe = qpe_ref[...]

    def pair(k, carry):
        wait(0)
        start(b, 2 * k + 1, 1)
        attend(0)
        wait(1)
        last = k == n_chunks // 2 - 1
        start(jnp.where(last, jnp.minimum(b + 1, nb - 1), b), jnp.where(last, 0, 2 * k + 2), 0)
        attend(1)
        return carry

    def attend(slot):
        for g in range(groups):
            attend_group(slot, g)

    def attend_group(slot, g):
        s_parts, lat_pairs = [], []
        for n in range(ppg // 2):
            i0 = g * ppg + 2 * n
            latb = jnp.concatenate([lat_buf[slot, i0].astype(BF16), lat_buf[slot, i0 + 1].astype(BF16)], axis=0)
            kpe_t = jnp.concatenate([kr_buf[slot, i0], kr_buf[slot, i0 + 1]], axis=1).astype(BF16)
            ksc_t = jnp.concatenate([ks_buf[slot, i0], ks_buf[slot, i0 + 1]], axis=1)
            s = _dot(latb, qabs_t).T[:B_HEADS] + _dot(qpe, kpe_t)
            s_parts.append(s * (ksc_t * scale))
            lat_pairs.append(latb)
        s = jnp.concatenate(s_parts, axis=1)
        m_old = m_ref[g]
        m_new = jnp.maximum(m_old, jnp.max(s, axis=-1, keepdims=True))
        alpha = jnp.exp(m_old - m_new)
        p = jnp.exp(s - m_new)
        l_ref[g] = alpha * l_ref[g] + jnp.sum(p, axis=-1, keepdims=True)
        pb = p.astype(BF16)
        acc = alpha * acc_ref[g]
        w = lat_pairs[0].shape[0]
        for n in range(ppg // 2):
            acc = acc + _dot(pb[:, n * w:(n + 1) * w], lat_pairs[n])
        acc_ref[g] = acc
        m_ref[g] = m_new

    lax.fori_loop(0, n_chunks // 2, pair, 0)

    @pl.when(b == nb - 1)
    def _():
        wait(0)

    latn = latn_ref[...]
    s_new = (jnp.sum(qabs_ref[...].astype(F32) * latn.astype(BF16).astype(F32), axis=-1, keepdims=True)
             + jnp.sum(qpe.astype(F32) * kpen_ref[...].astype(BF16).astype(F32), axis=-1, keepdims=True))
    s_new = s_new * (kscn_ref[...] * scale)
    m_f = s_new
    for g in range(groups):
        m_f = jnp.maximum(m_f, m_ref[g])
    p_new = jnp.exp(s_new - m_f)
    den = p_new
    num = p_new * latn
    for g in range(groups):
        a = jnp.exp(m_ref[g] - m_f)
        den = den + a * l_ref[g]
        num = num + a * acc_ref[g]
    o_ref[...] = num / den


def _mla_decode(page_table, c_lat, c_kr_t, c_ks_t, qabs_t, qabs, qpe, latn, kpen, kscn, pages_per_step=32, groups=1):
    nseq, n_pages = page_table.shape
    page = c_lat.shape[1]
    pps = min(pages_per_step, n_pages // 2)
    n_chunks = n_pages // pps
    assert n_pages % pps == 0 and pps % (2 * groups) == 0 and n_chunks % 2 == 0

    hbm = pl.BlockSpec(memory_space=pl.ANY)
    seq3 = lambda r, w: pl.BlockSpec((None, r, w), lambda b, pt: (b, 0, 0))
    grid_spec = pltpu.PrefetchScalarGridSpec(
        num_scalar_prefetch=1,
        grid=(nseq,),
        in_specs=[hbm, hbm, hbm, seq3(KV_LORA, LANES), seq3(B_HEADS, KV_LORA), seq3(B_HEADS, ROPE_DIM),
                  seq3(1, KV_LORA), seq3(1, ROPE_DIM), seq3(B_HEADS, 1)],
        out_specs=seq3(B_HEADS, KV_LORA),
        scratch_shapes=[pltpu.VMEM((2, pps, page, KV_LORA), F32), pltpu.VMEM((2, pps, ROPE_DIM, page), F32),
                        pltpu.VMEM((2, pps, B_HEADS, page), F32), pltpu.SemaphoreType.DMA((2, 3)),
                        pltpu.VMEM((groups, B_HEADS, 1), F32), pltpu.VMEM((groups, B_HEADS, 1), F32),
                        pltpu.VMEM((groups, B_HEADS, KV_LORA), F32)],
    )
    return pl.pallas_call(
        functools.partial(_mla_decode_kernel, pages_per_step=pps, groups=groups, n_chunks=n_chunks),
        grid_spec=grid_spec,
        out_shape=jax.ShapeDtypeStruct((nseq, B_HEADS, KV_LORA), F32),
        compiler_params=_cparams(("arbitrary",)),
        name="mla_decode",
    )(page_table, c_lat, c_kr_t, c_ks_t, qabs_t, qabs, qpe, latn, kpen, kscn)


def _head_nt_kernel(a_ref, w_ref, o_ref):
    o_ref[...] = _dot_nt(a_ref[...], w_ref[...])


def _absorb_q(qm_s, wuk):
    nseq = qm_s.shape[0]
    return pl.pallas_call(
        _head_nt_kernel,
        grid=(B_HEADS,),
        in_specs=[pl.BlockSpec((nseq, NOPE_DIM), lambda h: (0, 2 * h)),
                  pl.BlockSpec((KV_LORA, NOPE_DIM), lambda h: (0, h))],
        out_specs=pl.BlockSpec((None, nseq, KV_LORA), lambda h: (h, 0, 0)),
        out_shape=jax.ShapeDtypeStruct((B_HEADS, nseq, KV_LORA), F32),
        compiler_params=_cparams(("parallel",)),
        name="absorb_q",
    )(qm_s, wuk)


def _head_nn_kernel(a_ref, w_ref, o_ref):
    o_ref[...] = _dot(a_ref[...].astype(BF16), w_ref[...]).astype(o_ref.dtype)


def _expand_o(o_lat_h, wuv):
    _, nseq, _ = o_lat_h.shape
    return pl.pallas_call(
        _head_nn_kernel,
        grid=(B_HEADS,),
        in_specs=[pl.BlockSpec((None, nseq, KV_LORA), lambda h: (h, 0, 0)),
                  pl.BlockSpec((KV_LORA, V_DIM), lambda h: (0, h))],
        out_specs=pl.BlockSpec((nseq, V_DIM), lambda h: (0, h)),
        out_shape=jax.ShapeDtypeStruct((nseq, B_HEADS * V_DIM), BF16),
        compiler_params=_cparams(("parallel",)),
        name="expand_o",
    )(o_lat_h, wuv)


def _merge_kernel(oa_ref, ob_ref, ga_ref, gb_ref, wa_ref, wb_ref, o_ref):
    a = jax.nn.sigmoid(ga_ref[...]) * _dot(oa_ref[...], wa_ref[...])
    b = jax.nn.sigmoid(gb_ref[...]) * _dot(ob_ref[...], wb_ref[...])
    o_ref[...] = (a + b).astype(o_ref.dtype)


def _merge(oa, ob, proj, wa, wb, tm):
    n = oa.shape[0]
    d = wa.shape[1]
    return pl.pallas_call(
        _merge_kernel,
        grid=(n // tm,),
        in_specs=[pl.BlockSpec((tm, oa.shape[1]), lambda i: (i, 0)), pl.BlockSpec((tm, ob.shape[1]), lambda i: (i, 0)),
                  pl.BlockSpec((tm, d), lambda i: (i, COL_GA // d)), pl.BlockSpec((tm, d), lambda i: (i, COL_GB // d)),
                  pl.BlockSpec(wa.shape, lambda i: (0, 0)), pl.BlockSpec(wb.shape, lambda i: (0, 0))],
        out_specs=pl.BlockSpec((tm, d), lambda i: (i, 0)),
        out_shape=jax.ShapeDtypeStruct((n, d), BF16),
        compiler_params=_cparams(("parallel",)),
        name="merge",
    )(oa, ob, proj, proj, wa, wb)


def _outproj_router_kernel(t_ref, x_ref, wo_ref, g_ref, wr_ref, rb_ref, h_ref, hn_ref, ti_ref, tw_ref):
    h = x_ref[...] + _dot(t_ref[...], wo_ref[...])
    h_ref[...] = h
    inv = lax.rsqrt(jnp.mean(h * h, axis=-1, keepdims=True) + EPS)
    hn = h * inv * g_ref[...]
    hn_ref[...] = hn
    scores = jax.nn.sigmoid(_dot(hn.astype(BF16), wr_ref[...]))
    lane = _lane_iota(scores.shape)
    sel = jnp.where(lane < N_EXPERTS, scores + rb_ref[...], NEG)
    lane_f = lane.astype(F32)
    ti = jnp.zeros(scores.shape, F32)
    tw = jnp.zeros(scores.shape, F32)
    for r in range(TOP_K):
        mx = jnp.max(sel, axis=-1, keepdims=True)
        idx = jnp.min(jnp.where(sel == mx, lane_f, float(LANES)), axis=-1, keepdims=True)
        pick = lane_f == idx
        w = jnp.sum(jnp.where(pick, scores, 0.0), axis=-1, keepdims=True)
        ti = jnp.where(lane == r, idx, ti)
        tw = jnp.where(lane == r, w, tw)
        sel = jnp.where(pick, -3e38, sel)
    tw = tw / jnp.sum(tw, axis=-1, keepdims=True) * ROUTED_SCALE
    ti_ref[...] = ti[:, :8].astype(jnp.int32)
    tw_ref[...] = tw[:, :8]


def _outproj_router(t, x, wo, g, wr, rb, tm):
    n, d = x.shape
    row = lambda w: pl.BlockSpec((tm, w), lambda i: (i, 0))
    const = lambda a: pl.BlockSpec(a.shape, lambda i: (0,) * a.ndim)
    return pl.pallas_call(
        _outproj_router_kernel,
        grid=(n // tm,),
        in_specs=[row(d), row(d), const(wo), const(g), const(wr), const(rb)],
        out_specs=[row(d), row(d), row(8), row(8)],
        out_shape=[jax.ShapeDtypeStruct((n, d), F32), jax.ShapeDtypeStruct((n, d), F32),
                   jax.ShapeDtypeStruct((n, 8), jnp.int32), jax.ShapeDtypeStruct((n, 8), F32)],
        compiler_params=_cparams(("parallel",)),
        name="outproj_router",
    )(t, x, wo, g, wr, rb)


def _experts_kernel(be_ref, nv_ref, x_ref, wg_ref, wu_ref, wd_ref, o_ref):
    del be_ref
    b = pl.program_id(0)

    @pl.when(b < nv_ref[0])
    def _():
        x = x_ref[...].astype(BF16)
        hid = jax.nn.silu(_dot(x, wg_ref[...].astype(BF16))) * _dot(x, wu_ref[...].astype(BF16))
        o_ref[...] = _dot(hid.astype(BF16), wd_ref[...].astype(BF16))

    @pl.when(b >= nv_ref[0])
    def _():
        o_ref[...] = jnp.zeros(o_ref.shape, o_ref.dtype)


def _experts(blk_expert, n_valid, xs, wg, wu, wd, tm):
    rows, d = xs.shape
    de = wg.shape[2]
    grid_spec = pltpu.PrefetchScalarGridSpec(
        num_scalar_prefetch=2,
        grid=(rows // tm,),
        in_specs=[pl.BlockSpec((tm, d), lambda b, be, nv: (b, 0)),
                  pl.BlockSpec((None, d, de), lambda b, be, nv: (be[b], 0, 0)),
                  pl.BlockSpec((None, d, de), lambda b, be, nv: (be[b], 0, 0)),
                  pl.BlockSpec((None, de, d), lambda b, be, nv: (be[b], 0, 0))],
        out_specs=pl.BlockSpec((tm, d), lambda b, be, nv: (b, 0)),
    )
    return pl.pallas_call(
        _experts_kernel,
        grid_spec=grid_spec,
        out_shape=jax.ShapeDtypeStruct((rows, d), F32),
        compiler_params=_cparams(("arbitrary",)),
        name="experts",
    )(blk_expert, n_valid, xs, wg, wu, wd)


def _shared_combine_kernel(hn_ref, h_ref, y_ref, tw_ref, wg_ref, wu_ref, wd_ref, o_ref):
    x = hn_ref[...].astype(BF16)
    hid = jax.nn.silu(_dot(x, wg_ref[...])) * _dot(x, wu_ref[...])
    acc = h_ref[...] + _dot(hid.astype(BF16), wd_ref[...])
    tw = tw_ref[...]
    for k in range(TOP_K):
        acc = acc + tw[:, k:k + 1] * y_ref[k]
    o_ref[...] = acc


def _shared_combine(hn, h, y_slots, tw, wg, wu, wd, tm):
    n, d = h.shape
    row = pl.BlockSpec((tm, d), lambda i: (i, 0))
    const = lambda a: pl.BlockSpec(a.shape, lambda i: (0, 0))
    return pl.pallas_call(
        _shared_combine_kernel,
        grid=(n // tm,),
        in_specs=[row, row, pl.BlockSpec((TOP_K, tm, d), lambda i: (0, i, 0)),
                  pl.BlockSpec((tm, tw.shape[1]), lambda i: (i, 0)), const(wg), const(wu), const(wd)],
        out_specs=row,
        out_shape=jax.ShapeDtypeStruct((n, d), F32),
        compiler_params=_cparams(("parallel",)),
        name="shared_combine",
    )(hn, h, y_slots, tw, wg, wu, wd)


def _rel_bucket(dist):
    dist = jnp.maximum(dist, 0)
    max_exact = REL_BUCKETS // 2
    d = jnp.maximum(dist, 1).astype(F32)
    large = max_exact + (jnp.log(d / max_exact) / math.log(REL_MAX_DIST / max_exact)
                         * (REL_BUCKETS - max_exact)).astype(jnp.int32)
    large = jnp.minimum(large, REL_BUCKETS - 1)
    return jnp.where(dist < max_exact, dist, large)


def _rope_table(pos):
    half = ROPE_DIM // 2
    freqs = jnp.power(ROPE_THETA, -jnp.arange(half, dtype=F32) / half)
    ang = pos.astype(F32)[:, None] * freqs
    cos, sin = jnp.cos(ang), jnp.sin(ang)
    return jnp.concatenate([cos, cos, -sin, sin], axis=-1)


def _prepare(w_in, a_q_norm, a_k_norm, b_q_a_norm, b_q_up, b_kv_a_norm, b_w_uk, b_w_uv, b_q_norm, b_k_norm):
    d = w_in.shape[0]
    half = ROPE_DIM // 2
    swap = lambda t: jnp.concatenate([t[..., half:], t[..., :half]], axis=-1)
    o = np.cumsum([0, A_Q_W, A_KV_W, A_KV_W, Q_LORA, KV_LORA, ROPE_DIM, d, d])
    aq, ak, av, bq, bkv, bkr, ga, gb = [w_in[:, o[i]:o[i + 1]] for i in range(8)]
    wcat = jnp.concatenate([ga, gb, aq, ak, av, bq, bkv, bkr, swap(bkr), jnp.zeros((d, LANES), F32)], axis=1)
    assert wcat.shape[1] == W_CAT_COLS
    q_pe_w = b_q_up[:, :, NOPE_DIM:]
    wq = jnp.concatenate([b_q_up[:, :, :NOPE_DIM], q_pe_w, swap(q_pe_w)], axis=-1)
    gq_head = jnp.concatenate([b_q_norm[:NOPE_DIM] * b_k_norm[:NOPE_DIM], b_q_norm[NOPE_DIM:], swap(b_q_norm[NOPE_DIM:])])
    return {
        "wcat": wcat.astype(BF16),
        "wq": wq.reshape(Q_LORA, B_HEADS * MLA_HEAD_PAD).astype(BF16),
        "wuk": b_w_uk.reshape(KV_LORA, B_HEADS * NOPE_DIM).astype(BF16),
        "wuv": b_w_uv.reshape(KV_LORA, B_HEADS * V_DIM).astype(BF16),
        "gaq": jnp.tile(a_q_norm, A_HEADS)[None, :],
        "gak": jnp.tile(a_k_norm, A_KV_HEADS)[None, :],
        "gbq": b_q_a_norm[None, :],
        "gbkv": b_kv_a_norm[None, :],
        "gq": jnp.tile(gq_head, B_HEADS)[None, :],
        "gk": jnp.concatenate([b_k_norm[NOPE_DIM:], swap(b_k_norm[NOPE_DIM:])])[None, :],
    }


def _toeplitz(w):
    hh, two_n = w.shape
    n = two_n // 2
    flat = jnp.tile(w, (1, n))[:, :n * (two_n - 1)]
    return flat.reshape(hh, n, two_n - 1)[:, :, :n]


def _bias_tables(rel_bias):
    dist = jnp.arange(2 * MOBA_BLOCK, dtype=jnp.int32)
    tbl = rel_bias[:, _rel_bucket(dist)] - rel_bias[:, REL_BUCKETS - 1:]
    n = MOBA_BLOCK
    hh = tbl.shape[0]
    rev = lambda t: t[:, ::-1]
    fill = jnp.zeros((hh, 1), F32)
    tdiag = _toeplitz(jnp.concatenate([tbl[:, :1], jnp.full((hh, n - 1), NEG, F32), fill, rev(tbl[:, 1:n])], axis=1))
    tprev = _toeplitz(jnp.concatenate([rev(tbl[:, 1:n + 1]), fill, rev(tbl[:, n + 1:])], axis=1))
    tlast = rev(tbl[:, 1:n + 1])
    return tdiag, tprev, tlast, tbl[:, :1]


def _dispatch(topi, tm):
    n = topi.shape[0]
    a = n * TOP_K
    e_flat = topi[:, :TOP_K].reshape(a)
    oh = (e_flat[:, None] == jnp.arange(N_EXPERTS, dtype=jnp.int32)[None, :]).astype(jnp.int32)
    csum = jnp.cumsum(oh, axis=0)
    counts = csum[-1]
    rank = jnp.sum(oh * csum, axis=1) - 1
    padded = (counts + tm - 1) // tm * tm
    pad_end = jnp.cumsum(padded)
    pad_start = pad_end - padded
    dest = jnp.sum(oh * pad_start[None, :], axis=1) + rank
    nb = -(-a // tm) + N_EXPERTS
    tok = jnp.arange(a, dtype=jnp.int32) // TOP_K
    row_tok = jnp.zeros((nb * tm,), jnp.int32).at[dest].set(tok)
    blk_expert = jnp.minimum(jnp.searchsorted(pad_end, jnp.arange(nb, dtype=jnp.int32) * tm, side="right"),
                             N_EXPERTS - 1).astype(jnp.int32)
    n_valid = (pad_end[-1:] // tm).astype(jnp.int32)
    return dest.reshape(n, TOP_K), row_tok, blk_expert, n_valid


def _tile(n, pref):
    t = pref
    while n % t:
        t //= 2
    return t


def kernel(x_prompt, x_sample, cache_moba_k, cache_moba_v, cache_mla_latent, cache_mla_krope, cache_mla_kscale,
           page_table, rel_bias, attn_norm, w_in, a_q_norm, a_k_norm, b_q_a_norm, b_q_up, b_kv_a_norm, b_w_uk,
           b_w_uv, b_q_norm, b_k_norm, w_branch_a, w_branch_b, w_out, ffn_norm, w_router, router_bias,
           w_e_gate, w_e_up, w_e_down, w_s_gate, w_s_up, w_s_down):
    depth = w_in.shape[0]
    assert depth == 1
    b_p, s_p, d = x_prompt.shape
    b_d, s_d, _ = x_sample.shape
    assert b_p == 1 and s_d == 1 and s_p % MOBA_BLOCK == 0
    pool, page = cache_moba_k.shape[1], cache_moba_k.shape[2]
    n_pages = page_table.shape[1]
    past = n_pages * page
    assert past % MOBA_BLOCK == 0 and MOBA_BLOCK % page == 0
    l = 0

    prm = _prepare(w_in[l], a_q_norm[l], a_k_norm[l], b_q_a_norm[l], b_q_up[l], b_kv_a_norm[l], b_w_uk[l],
                   b_w_uv[l], b_q_norm[l], b_k_norm[l])
    tdiag, tprev, tlast, bias0 = _bias_tables(rel_bias)
    g_attn = attn_norm[l][None, :]
    wa = w_branch_a[l].astype(BF16)
    wb = w_branch_b[l].astype(BF16)
    wo = w_out[l].astype(BF16)
    g_ffn = ffn_norm[l][None, :]
    wr = jnp.pad(w_router[l], ((0, 0), (0, LANES - N_EXPERTS))).astype(BF16)
    rb = jnp.pad(router_bias[l], (0, LANES - N_EXPERTS))[None, :]
    wsg, wsu, wsd = w_s_gate[l].astype(BF16), w_s_up[l].astype(BF16), w_s_down[l].astype(BF16)

    def token_front(x, pos, tm_front):
        n = x.shape[0]
        tm = _tile(n, 512)
        xn = _rmsnorm_bf16(x, g_attn, tm)
        proj = _matmul(xn, prm["wcat"], tm, W_CAT_COLS // 6)
        return proj, _front(proj, _rope_table(pos), prm, tm_front)

    def token_back(x, proj, oa, ob):
        n = x.shape[0]
        tm = _tile(n, 512)
        t = _merge(oa, ob, proj, wa, wb, tm)
        return _outproj_router(t, x, wo, g_ffn, wr, rb, _tile(n, 256))

    xp = x_prompt.reshape(s_p, d)
    proj_p, fp = token_front(xp, jnp.arange(s_p, dtype=jnp.int32), MOBA_BLOCK)
    qa, ka, va, kaug, vbf, kmean, qm, km, vm, lat, kpe, ksc = fp
    nblk = s_p // MOBA_BLOCK
    assert nblk <= LANES
    kmean_pad = jnp.pad(kmean.reshape(nblk, A_KV_HEADS, A_HEAD_DIM).transpose(1, 0, 2),
                        ((0, 0), (0, LANES - nblk), (0, 0)))
    tdiag_g = tdiag.reshape(A_KV_HEADS * A_GROUP, MOBA_BLOCK, MOBA_BLOCK)
    oa_p = _moba_prompt(qa, kaug, vbf, kmean_pad, tdiag_g, tprev)
    tq_mla = _tile(s_p, 2048)
    ob_p = _mla_prompt(qm, km, vm, tq_mla, tq_mla // 256)
    h_p, hn_p, ti_p, tw_p = token_back(xp, proj_p, oa_p, ob_p)

    xd = x_sample.reshape(b_d, d)
    pos_d = jnp.full((b_d,), past, jnp.int32)
    proj_d, fd = token_front(xd, pos_d, _tile(b_d, 256))
    qa_d, ka_d, va_d, _, _, _, qm_d, _, _, lat_d, kpe_d, ksc_d = fd
    ck = cache_moba_k[l].reshape(pool, page * A_KV_HEADS, A_HEAD_DIM)
    cv = cache_moba_v[l].reshape(pool, page * A_KV_HEADS, A_HEAD_DIM)
    rep = lambda t: jnp.repeat(t.reshape(b_d, A_KV_HEADS, A_HEAD_DIM), A_GROUP, axis=1)
    sel = _moba_gate(page_table, ck, qa_d.reshape(b_d, A_HEADS, A_HEAD_DIM))[:, :, :MOBA_TOPK]
    ppb = MOBA_BLOCK // page
    sel_pg = (sel[..., None] * ppb + jnp.arange(ppb, dtype=jnp.int32)).reshape(b_d, -1)
    sel_pages = jnp.take_along_axis(page_table, sel_pg, axis=1).reshape(-1)
    oa_d = _moba_attend(sel_pages, sel.reshape(-1), ck, cv, qa_d.reshape(b_d, A_HEADS, A_HEAD_DIM), rep(ka_d),
                        rep(va_d), tlast[:, :, None], bias0[:, :, None], past // MOBA_BLOCK)
    oa_d = oa_d.reshape(b_d, A_Q_W).astype(BF16)

    qabs_h = _absorb_q(qm_d, prm["wuk"])
    qabs = qabs_h.transpose(1, 0, 2).astype(BF16)
    qabs_t = jnp.pad(qabs.transpose(0, 2, 1), ((0, 0), (0, 0), (0, LANES - B_HEADS)))
    qpe = qm_d.reshape(b_d, B_HEADS, MLA_HEAD_PAD)[:, :, NOPE_DIM:NOPE_DIM + ROPE_DIM]
    c_kr_t = jnp.swapaxes(cache_mla_krope[l], 1, 2)
    c_ks_t = jnp.swapaxes(cache_mla_kscale[l], 1, 2)
    o_lat = _mla_decode(page_table, cache_mla_latent[l], c_kr_t, c_ks_t, qabs_t, qabs, qpe,
                        lat_d[:, None, :], kpe_d[:, None, :], ksc_d[:, :, None])
    ob_d = _expand_o(o_lat.transpose(1, 0, 2), prm["wuv"])
    h_d, hn_d, ti_d, tw_d = token_back(xd, proj_d, oa_d, ob_d)

    n_all = s_p + b_d
    hn_all = jnp.concatenate([hn_p, hn_d], axis=0)
    ti_all = jnp.concatenate([ti_p, ti_d], axis=0)
    tw_all = jnp.concatenate([tw_p, tw_d], axis=0)
    del n_all, tw_all
    tm_e = 256
    dest, row_tok, blk_expert, n_valid = _dispatch(ti_all, tm_e)
    xs = hn_all[row_tok]
    yb = _experts(blk_expert, n_valid, xs, w_e_gate[l], w_e_up[l], w_e_down[l], tm_e)
    y_p = _shared_combine(hn_p, h_p, yb[dest[:s_p].T], tw_p, wsg, wsu, wsd, _tile(s_p, 128))
    y_d = _shared_combine(hn_d, h_d, yb[dest[s_p:].T], tw_d, wsg, wsu, wsd, _tile(b_d, 128))

    y_prompt = y_p.reshape(b_p, s_p, d)
    y_sample = y_d.reshape(b_d, s_d, d)
    return (y_prompt, y_sample,
            ka.reshape(1, b_p, s_p, A_KV_HEADS, A_HEAD_DIM), va.reshape(1, b_p, s_p, A_KV_HEADS, A_HEAD_DIM),
            lat.reshape(1, b_p, s_p, KV_LORA), kpe.reshape(1, b_p, s_p, ROPE_DIM), ksc.reshape(1, b_p, s_p, B_HEADS),
            ka_d.reshape(1, b_d, s_d, A_KV_HEADS, A_HEAD_DIM), va_d.reshape(1, b_d, s_d, A_KV_HEADS, A_HEAD_DIM),
            lat_d.reshape(1, b_d, s_d, KV_LORA), kpe_d.reshape(1, b_d, s_d, ROPE_DIM),
            ksc_d.reshape(1, b_d, s_d, B_HEADS))
```

```python
import functools
import math

import numpy as np
import jax
import jax.numpy as jnp
from jax import lax
from jax.experimental import pallas as pl
from jax.experimental.pallas import tpu as pltpu

F32 = jnp.float32
BF16 = jnp.bfloat16

A_HEADS = 8
A_KV_HEADS = 2
A_HEAD_DIM = 128
A_GROUP = A_HEADS // A_KV_HEADS
MOBA_BLOCK = 256
MOBA_TOPK = 3
MOBA_FAR_GROUP = 4
B_HEADS = 8
Q_LORA = 512
KV_LORA = 512
NOPE_DIM = 128
ROPE_DIM = 64
V_DIM = 128
B_QK_DIM = NOPE_DIM + ROPE_DIM
ROPE_THETA = 10000.0
REL_BUCKETS = 32
REL_MAX_DIST = 128
N_EXPERTS = 64
TOP_K = 6
ROUTED_SCALE = 1.0
EPS = 1e-6

LANES = 128
MLA_HEAD_PAD = 256
NEG = -1e30
VMEM_LIMIT = 56 * 1024 * 1024

A_Q_W = A_HEADS * A_HEAD_DIM
A_KV_W = A_KV_HEADS * A_HEAD_DIM

COL_GA = 0
COL_GB = 2048
COL_AQ = 4096
COL_AK = 5120
COL_AV = 5376
COL_BQ = 5632
COL_BKV = 6144
COL_BKR = 6656
W_CAT_COLS = 6912


def _cparams(sem):
    return pltpu.CompilerParams(dimension_semantics=sem, vmem_limit_bytes=VMEM_LIMIT)


def _dot(a, b):
    return jnp.dot(a, b, preferred_element_type=F32)


def _dot_nt(a, b):
    return lax.dot_general(a, b, (((1,), (1,)), ((), ())), preferred_element_type=F32)


def _split_bf16(a):
    hi = a.astype(BF16)
    lo = (a - hi.astype(F32)).astype(BF16)
    return hi, lo


def _dot3(a, b):
    ah, al = _split_bf16(a)
    bh, bl = _split_bf16(b)
    return _dot(ah, bh) + (_dot(ah, bl) + _dot(al, bh))


def _lane_iota(shape):
    return lax.broadcasted_iota(jnp.int32, shape, len(shape) - 1)


def _rmsnorm_kernel(x_ref, g_ref, o_ref):
    x = x_ref[...]
    inv = lax.rsqrt(jnp.mean(x * x, axis=-1, keepdims=True) + EPS)
    o_ref[...] = (x * inv * g_ref[...]).astype(o_ref.dtype)


def _rmsnorm_bf16(x, g, tm):
    n, d = x.shape
    return pl.pallas_call(
        _rmsnorm_kernel,
        grid=(n // tm,),
        in_specs=[pl.BlockSpec((tm, d), lambda i: (i, 0)), pl.BlockSpec((1, d), lambda i: (0, 0))],
        out_specs=pl.BlockSpec((tm, d), lambda i: (i, 0)),
        out_shape=jax.ShapeDtypeStruct((n, d), BF16),
        compiler_params=_cparams(("parallel",)),
        name="rmsnorm",
    )(x, g)


def _matmul_kernel(a_ref, b_ref, o_ref):
    o_ref[...] = _dot(a_ref[...], b_ref[...])


def _matmul(a, b, tm, tn):
    m, k = a.shape
    _, n = b.shape
    return pl.pallas_call(
        _matmul_kernel,
        grid=(n // tn, m // tm),
        in_specs=[pl.BlockSpec((tm, k), lambda j, i: (i, 0)), pl.BlockSpec((k, tn), lambda j, i: (0, j))],
        out_specs=pl.BlockSpec((tm, tn), lambda j, i: (i, j)),
        out_shape=jax.ShapeDtypeStruct((m, n), F32),
        compiler_params=_cparams(("parallel", "parallel")),
        name="in_proj",
    )(a, b)


def _front_kernel(aq_ref, ak_ref, av_ref, bq_ref, bkv_ref, bkr_ref, cs_ref,
                  gaq_ref, gak_ref, gbq_ref, gbkv_ref, gq_ref, gk_ref,
                  wq_ref, wuk_ref, wuv_ref,
                  qa_ref, ka_ref, va_ref, kaug_ref, vbf_ref, kmean_ref,
                  qm_ref, km_ref, vm_ref, lat_ref, kpe_ref, ksc_ref, *, tm):
    pid = pl.program_id(0)
    lane = _lane_iota((tm, LANES))
    lo = lane < ROPE_DIM
    cs = cs_ref[...]

    def head_norm(x, g):
        inv = lax.rsqrt(jnp.mean(x * x, axis=-1, keepdims=True) + EPS)
        return x * inv * g

    for h in range(A_HEADS):
        sl = slice(h * A_HEAD_DIM, (h + 1) * A_HEAD_DIM)
        qa_ref[:, sl] = head_norm(aq_ref[:, sl], gaq_ref[:, sl])
    row = lax.broadcasted_iota(jnp.int32, (tm, LANES), 0) + pid * tm
    onehot = jnp.where(lane == row // MOBA_BLOCK, 1.0, 0.0).astype(BF16)
    for g in range(A_KV_HEADS):
        sl = slice(g * A_HEAD_DIM, (g + 1) * A_HEAD_DIM)
        k = head_norm(ak_ref[:, sl], gak_ref[:, sl])
        ka_ref[:, sl] = k
        kmean_ref[0, :, sl] = jnp.mean(k, axis=0, keepdims=True)
        kaug_ref[:, 2 * g * LANES:(2 * g + 1) * LANES] = k.astype(BF16)
        kaug_ref[:, (2 * g + 1) * LANES:(2 * g + 2) * LANES] = onehot
    av = av_ref[...]
    va_ref[...] = av
    ones = jnp.ones((tm, LANES), BF16)
    for g in range(A_KV_HEADS):
        vbf_ref[:, 2 * g * LANES:(2 * g + 1) * LANES] = av[:, g * A_HEAD_DIM:(g + 1) * A_HEAD_DIM].astype(BF16)
        vbf_ref[:, (2 * g + 1) * LANES:(2 * g + 2) * LANES] = ones

    qc = head_norm(bq_ref[...], gbq_ref[...]).astype(BF16)
    qraw = _dot(qc, wq_ref[...])
    for h in range(B_HEADS):
        a = qraw[:, h * MLA_HEAD_PAD:h * MLA_HEAD_PAD + LANES]
        b = qraw[:, h * MLA_HEAD_PAD + LANES:(h + 1) * MLA_HEAD_PAD]
        ssq = jnp.sum(a * a, axis=-1, keepdims=True) + jnp.sum(jnp.where(lo, b * b, 0.0), axis=-1, keepdims=True)
        inv = lax.rsqrt(ssq / B_QK_DIM + EPS)
        qm_ref[:, h * MLA_HEAD_PAD:h * MLA_HEAD_PAD + LANES] = (
            a * inv * gq_ref[:, h * MLA_HEAD_PAD:h * MLA_HEAD_PAD + LANES]).astype(BF16)
        t = b * inv * gq_ref[:, h * MLA_HEAD_PAD + LANES:(h + 1) * MLA_HEAD_PAD] * cs
        pe = jnp.where(lo, t + pltpu.roll(t, ROPE_DIM, axis=1), 0.0)
        qm_ref[:, h * MLA_HEAD_PAD + LANES:(h + 1) * MLA_HEAD_PAD] = pe.astype(BF16)

    lat = head_norm(bkv_ref[...], gbkv_ref[...])
    lat_ref[...] = lat
    latb = lat.astype(BF16)
    knope = _dot(latb, wuk_ref[...])
    vb = _dot(latb, wuv_ref[...]).astype(BF16)
    for h in range(B_HEADS):
        vm_ref[:, 2 * h * LANES:(2 * h + 1) * LANES] = vb[:, h * V_DIM:(h + 1) * V_DIM]
        vm_ref[:, (2 * h + 1) * LANES:(2 * h + 2) * LANES] = ones
    bkr = bkr_ref[...]
    rss = jnp.sum(jnp.where(lo, bkr * bkr, 0.0), axis=-1, keepdims=True)
    t = bkr * gk_ref[...] * cs
    kpe = jnp.where(lo, t + pltpu.roll(t, ROPE_DIM, axis=1), 0.0)
    kpe_ref[...] = kpe[:, :ROPE_DIM]
    ksw = jnp.zeros((tm, LANES), F32)
    scale = B_QK_DIM ** -0.5
    for h in range(B_HEADS):
        kh = knope[:, h * NOPE_DIM:(h + 1) * NOPE_DIM]
        ssq = jnp.sum(kh * kh, axis=-1, keepdims=True) + rss
        ks = lax.rsqrt(ssq / B_QK_DIM + EPS)
        ksw = jnp.where(lane == h, ks, ksw)
        km_ref[:, h * MLA_HEAD_PAD:h * MLA_HEAD_PAD + LANES] = (kh * (ks * scale)).astype(BF16)
        km_ref[:, h * MLA_HEAD_PAD + LANES:(h + 1) * MLA_HEAD_PAD] = (kpe * (ks * scale)).astype(BF16)
    ksc_ref[...] = ksw[:, :B_HEADS]


def _front(proj, cs, prm, tm):
    n = proj.shape[0]
    nt = n // tm
    row = lambda w, c: pl.BlockSpec((tm, w), lambda i, c=c, w=w: (i, c // w))
    const = lambda a: pl.BlockSpec(a.shape, lambda i: (0,) * a.ndim)
    consts = [prm["gaq"], prm["gak"], prm["gbq"], prm["gbkv"], prm["gq"], prm["gk"], prm["wq"], prm["wuk"], prm["wuv"]]
    out_shapes = [
        ((n, A_Q_W), F32), ((n, A_KV_W), F32), ((n, A_KV_W), F32),
        ((n, 2 * A_KV_W), BF16), ((n, 2 * A_KV_W), BF16),
        ((nt, 1, A_KV_W), F32),
        ((n, B_HEADS * MLA_HEAD_PAD), BF16), ((n, B_HEADS * MLA_HEAD_PAD), BF16),
        ((n, B_HEADS * 2 * V_DIM), BF16),
        ((n, KV_LORA), F32), ((n, ROPE_DIM), F32), ((n, B_HEADS), F32),
    ]
    out_specs = []
    for shp, _ in out_shapes:
        if len(shp) == 3:
            out_specs.append(pl.BlockSpec((1, 1, shp[2]), lambda i: (i, 0, 0)))
        else:
            out_specs.append(pl.BlockSpec((tm, shp[1]), lambda i: (i, 0)))
    return pl.pallas_call(
        functools.partial(_front_kernel, tm=tm),
        grid=(nt,),
        in_specs=[row(A_Q_W, COL_AQ), row(A_KV_W, COL_AK), row(A_KV_W, COL_AV), row(Q_LORA, COL_BQ),
                  row(KV_LORA, COL_BKV), row(LANES, COL_BKR), pl.BlockSpec((tm, LANES), lambda i: (i, 0))]
                 + [const(a) for a in consts],
        out_specs=out_specs,
        out_shape=[jax.ShapeDtypeStruct(s, d) for s, d in out_shapes],
        compiler_params=_cparams(("parallel",)),
        name="front",
    )(proj, proj, proj, proj, proj, proj, cs, *consts)


def _top_blocks(gate, valid, lane):
    g = jnp.where(valid, gate, NEG)
    lane_f = lane.astype(F32)
    selv = jnp.zeros(gate.shape, F32)
    for _ in range(MOBA_TOPK):
        mx = jnp.max(g, axis=-1, keepdims=True)
        idx = jnp.min(jnp.where(g == mx, lane_f, float(LANES)), axis=-1, keepdims=True)
        pick = lane_f == idx
        selv = jnp.where(pick, 1.0, selv)
        g = jnp.where(pick, -3e38, g)
    return jnp.where(valid, selv, 0.0)


def _rep(x, width):
    return x if width == LANES else jnp.concatenate([x] * (width // LANES), axis=1)


def _flash_first(s, v, m_ref, acc_ref, idx):
    m = jnp.max(s, axis=-1, keepdims=True)
    p = jnp.exp(s - m)
    m_ref[idx] = jnp.broadcast_to(m, (s.shape[0], LANES))
    acc_ref[idx] = _dot(p.astype(BF16), v)


def _flash_next(s, v, m_ref, acc_ref, idx):
    m_old = m_ref[idx]
    m_new = jnp.maximum(m_old, jnp.max(s, axis=-1, keepdims=True))
    alpha = jnp.exp(m_old - m_new)
    p = jnp.exp(s - _rep(m_new, s.shape[-1]))
    acc_ref[idx] = _rep(alpha, 2 * LANES) * acc_ref[idx] + _dot(p.astype(BF16), v)
    m_ref[idx] = m_new


def _flash_out(acc):
    return acc[:, :LANES] / acc[:, LANES:]


def _moba_prompt_kernel(q_ref, kaug_ref, v_ref, kmean_ref, tdiag_ref, tprev_ref, o_ref,
                        qaug_ref, qfar_ref, m_ref, acc_ref):
    i = pl.program_id(1)
    blk = MOBA_BLOCK
    lane = _lane_iota((blk, LANES))
    valid = lane < i
    kmean = kmean_ref[...]
    scale = A_HEAD_DIM ** -0.5

    diag0 = pl.multiple_of(i * blk, blk)
    kd = kaug_ref[pl.ds(diag0, blk), :]
    vd = v_ref[pl.ds(diag0, blk), :]
    for hh in range(A_GROUP):
        q = q_ref[:, hh * A_HEAD_DIM:(hh + 1) * A_HEAD_DIM]
        qh, ql = _split_bf16(q)
        kh, kl = _split_bf16(kmean)
        gate = _dot_nt(qh, kh) + (_dot_nt(qh, kl) + _dot_nt(ql, kh))
        selv = _top_blocks(gate, valid, lane)
        on = jnp.where(lane == i, 1.0, selv)
        qs = (q * scale).astype(BF16)
        qaug_ref[hh, :, :LANES] = qs
        qaug_ref[hh, :, LANES:] = jnp.where(on > 0.0, 0.0, NEG).astype(BF16)
        qfar_ref[hh, :, :LANES] = qs
        qfar_ref[hh, :, LANES:] = jnp.where(lane < i - 1, jnp.where(selv > 0.0, 0.0, NEG), NEG).astype(BF16)
        _flash_first(_dot_nt(qaug_ref[hh], kd) + tdiag_ref[hh], vd, m_ref, acc_ref, hh)

    @pl.when(i >= 1)
    def _():
        j0 = pl.multiple_of((i - 1) * blk, blk)
        kj = kaug_ref[pl.ds(j0, blk), :]
        vj = v_ref[pl.ds(j0, blk), :]
        for hh in range(A_GROUP):
            _flash_next(_dot_nt(qaug_ref[hh], kj) + tprev_ref[hh], vj, m_ref, acc_ref, hh)

    span = MOBA_FAR_GROUP * blk

    def far(g, carry):
        j0 = pl.multiple_of(g * span, span)
        kj = kaug_ref[pl.ds(j0, span), :]
        vj = v_ref[pl.ds(j0, span), :]
        for hh in range(A_GROUP):
            _flash_next(_dot_nt(qfar_ref[hh], kj), vj, m_ref, acc_ref, hh)
        return carry

    lax.fori_loop(0, (jnp.maximum(i - 1, 0) + MOBA_FAR_GROUP - 1) // MOBA_FAR_GROUP, far, 0)

    for hh in range(A_GROUP):
        o_ref[:, hh * A_HEAD_DIM:(hh + 1) * A_HEAD_DIM] = _flash_out(acc_ref[hh]).astype(o_ref.dtype)


def _moba_prompt(qa, kaug, vbf, kmean_pad, tdiag, tprev):
    s = qa.shape[0]
    nblk = s // MOBA_BLOCK
    assert nblk % MOBA_FAR_GROUP == 0
    gw = A_GROUP * A_HEAD_DIM
    return pl.pallas_call(
        _moba_prompt_kernel,
        grid=(A_KV_HEADS, nblk),
        in_specs=[
            pl.BlockSpec((MOBA_BLOCK, gw), lambda g, i: (i, g)),
            pl.BlockSpec((s, 2 * LANES), lambda g, i: (0, g)),
            pl.BlockSpec((s, 2 * LANES), lambda g, i: (0, g)),
            pl.BlockSpec((None, LANES, A_HEAD_DIM), lambda g, i: (g, 0, 0)),
            pl.BlockSpec((A_GROUP, MOBA_BLOCK, MOBA_BLOCK), lambda g, i: (g, 0, 0)),
            pl.BlockSpec((A_GROUP, MOBA_BLOCK, MOBA_BLOCK), lambda g, i: (g, 0, 0)),
        ],
        out_specs=pl.BlockSpec((MOBA_BLOCK, gw), lambda g, i: (i, g)),
        out_shape=jax.ShapeDtypeStruct((s, A_Q_W), BF16),
        scratch_shapes=[
            pltpu.VMEM((A_GROUP, MOBA_BLOCK, 2 * LANES), BF16),
            pltpu.VMEM((A_GROUP, MOBA_BLOCK, 2 * LANES), BF16),
            pltpu.VMEM((A_GROUP, MOBA_BLOCK, LANES), F32),
            pltpu.VMEM((A_GROUP, MOBA_BLOCK, 2 * LANES), F32),
        ],
        compiler_params=_cparams(("parallel", "parallel")),
        name="moba_prompt",
    )(qa, kaug, vbf, kmean_pad, tdiag, tprev)


def _mla_prompt_kernel(q_ref, k_ref, v_ref, o_ref, m_ref, acc_ref, *, tq, parts):
    qi = pl.program_id(1)
    rows = tq // parts
    d0 = pl.multiple_of(qi * tq, tq)
    kd = k_ref[pl.ds(d0, tq), :]
    vd = v_ref[pl.ds(d0, tq), :]
    c = lax.broadcasted_iota(jnp.int32, (rows, tq), 1)
    for a in range(parts):
        r = lax.broadcasted_iota(jnp.int32, (rows, tq), 0) + a * rows
        s = jnp.where(c <= r, _dot_nt(q_ref[a], kd), NEG)
        _flash_first(s, vd, m_ref, acc_ref, a)

    def body(j, carry):
        j0 = pl.multiple_of(j * tq, tq)
        kj = k_ref[pl.ds(j0, tq), :]
        vj = v_ref[pl.ds(j0, tq), :]
        for a in range(parts):
            _flash_next(_dot_nt(q_ref[a], kj), vj, m_ref, acc_ref, a)
        return carry

    lax.fori_loop(0, qi, body, 0)
    for a in range(parts):
        o_ref[a] = _flash_out(acc_ref[a]).astype(o_ref.dtype)


def _mla_prompt(qm, km, vm, tq, parts):
    s = qm.shape[0]
    rows = tq // parts
    out = pl.pallas_call(
        functools.partial(_mla_prompt_kernel, tq=tq, parts=parts),
        grid=(B_HEADS, s // tq),
        in_specs=[
            pl.BlockSpec((parts, rows, MLA_HEAD_PAD), lambda h, i: (i, 0, h)),
            pl.BlockSpec((s, MLA_HEAD_PAD), lambda h, i: (0, h)),
            pl.BlockSpec((s, 2 * V_DIM), lambda h, i: (0, h)),
        ],
        out_specs=pl.BlockSpec((parts, rows, V_DIM), lambda h, i: (i, 0, h)),
        out_shape=jax.ShapeDtypeStruct((s // rows, rows, B_HEADS * V_DIM), BF16),
        scratch_shapes=[pltpu.VMEM((parts, rows, LANES), F32), pltpu.VMEM((parts, rows, 2 * LANES), F32)],
        compiler_params=_cparams(("parallel", "parallel")),
        name="mla_prompt",
    )(qm.reshape(s // rows, rows, B_HEADS * MLA_HEAD_PAD), km, vm)
    return out.reshape(s, B_HEADS * V_DIM)


def _moba_gate_kernel(pt_ref, *refs, pages_per_step, n_blocks):
    del pt_ref
    pps = pages_per_step
    k_refs = refs[:pps]
    q_ref, sel_ref, g_all = refs[pps:]
    c = pl.program_id(1)
    nc = pl.num_programs(1)
    page = k_refs[0].shape[0] // A_KV_HEADS
    ppb = MOBA_BLOCK // page
    bps = pps // ppb
    lane = _lane_iota((A_HEADS, LANES))
    head = lax.broadcasted_iota(jnp.int32, (A_HEADS, LANES), 0)
    first_group = head < A_GROUP
    q = q_ref[...]

    @pl.when(c == 0)
    def _():
        g_all[...] = jnp.full((A_HEADS, LANES), NEG, F32)

    g_new = g_all[...]
    for nb in range(bps):
        ksum = []
        for g in range(A_KV_HEADS):
            part = None
            for pp in range(ppb):
                kp = k_refs[nb * ppb + pp][pl.ds(g, page, stride=A_KV_HEADS), :]
                part = kp if part is None else part + kp
            ksum.append(jnp.sum(part, axis=0, keepdims=True))
        kmean = jnp.where(first_group, ksum[0], ksum[1]) * (1.0 / MOBA_BLOCK)
        gate = jnp.sum(q * kmean, axis=-1, keepdims=True)
        g_new = jnp.where(lane == c * bps + nb, gate, g_new)
    g_all[...] = g_new

    @pl.when(c == nc - 1)
    def _():
        g = jnp.where(lane < n_blocks, g_all[...], NEG)
        lane_f = lane.astype(F32)
        sel = jnp.zeros((A_HEADS, LANES), F32)
        for r in range(MOBA_TOPK):
            mx = jnp.max(g, axis=-1, keepdims=True)
            idx = jnp.min(jnp.where(g == mx, lane_f, float(LANES)), axis=-1, keepdims=True)
            sel = jnp.where(lane == r, idx, sel)
            g = jnp.where(lane_f == idx, -3e38, g)
        sel_ref[...] = sel.astype(jnp.int32)


def _moba_gate(page_table, ck, q, pages_per_step=32):
    nseq, n_pages = page_table.shape
    pool, rows, _ = ck.shape
    page = rows // A_KV_HEADS
    n_blocks = n_pages * page // MOBA_BLOCK
    pps = min(pages_per_step, n_pages)
    assert n_pages % pps == 0 and MOBA_TOPK <= n_blocks <= LANES
    per_seq = pl.BlockSpec((None, A_HEADS, A_HEAD_DIM), lambda b, c, pt: (b, 0, 0))
    grid_spec = pltpu.PrefetchScalarGridSpec(
        num_scalar_prefetch=1,
        grid=(nseq, n_pages // pps),
        in_specs=[pl.BlockSpec((None, rows, A_HEAD_DIM), lambda b, c, pt, p=p: (pt[b, c * pps + p], 0, 0))
                  for p in range(pps)] + [per_seq],
        out_specs=per_seq,
        scratch_shapes=[pltpu.VMEM((A_HEADS, LANES), F32)],
    )
    return pl.pallas_call(
        functools.partial(_moba_gate_kernel, pages_per_step=pps, n_blocks=n_blocks),
        grid_spec=grid_spec,
        out_shape=jax.ShapeDtypeStruct((nseq, A_HEADS, LANES), jnp.int32),
        compiler_params=_cparams(("parallel", "arbitrary")),
        name="moba_gate",
    )(page_table, *([ck] * pps), q)


def _moba_attend_kernel(pg_ref, blk_ref, k_hbm, v_hbm, q_ref, knew_ref, vnew_ref, tlast_ref, bias0_ref, o_ref,
                        kbuf, vbuf, sem, *, n_sel_pages, n_blocks, page):
    nsp = n_sel_pages
    per_seq = A_HEADS * nsp
    t = pl.program_id(0)
    nt = pl.num_programs(0)
    ppb = MOBA_BLOCK // page
    scale = A_HEAD_DIM ** -0.5

    def seq_copies(seq, slot):
        out = []
        for i in range(per_seq):
            pid = pg_ref[seq * per_seq + i]
            out.append(pltpu.make_async_copy(k_hbm.at[pid], kbuf.at[slot, i], sem.at[slot, 0]))
            out.append(pltpu.make_async_copy(v_hbm.at[pid], vbuf.at[slot, i], sem.at[slot, 1]))
        return out

    def start(seq, slot):
        for cp in seq_copies(seq, slot):
            cp.start()

    def wait(slot):
        for cp in seq_copies(0, slot):
            cp.wait()

    def attend(seq, slot):
        for h in range(A_HEADS):
            g = h // A_GROUP
            qs = q_ref[slot, h:h + 1, :] * scale
            s_cols, v_pages = [], []
            for n in range(nsp):
                k = kbuf[slot, h * nsp + n, pl.ds(g, page, stride=A_KV_HEADS), :]
                s = jnp.sum(k * qs, axis=-1, keepdims=True)
                blk = blk_ref[(seq * A_HEADS + h) * MOBA_TOPK + n // ppb]
                bias = tlast_ref[h, pl.ds((n % ppb) * page, page), :]
                s_cols.append(s + jnp.where(blk == n_blocks - 1, bias, 0.0))
                v_pages.append(vbuf[slot, h * nsp + n, pl.ds(g, page, stride=A_KV_HEADS), :])
            s_new = jnp.sum(qs * knew_ref[slot, h:h + 1, :], axis=-1, keepdims=True) + bias0_ref[h]
            m = s_new
            for s in s_cols:
                m = jnp.maximum(m, jnp.max(s, axis=0, keepdims=True))
            p_new = jnp.exp(s_new - m)
            den = p_new
            acc = p_new * vnew_ref[slot, h:h + 1, :]
            for s, v in zip(s_cols, v_pages):
                p = jnp.exp(s - m)
                den = den + jnp.sum(p, axis=0, keepdims=True)
                acc = acc + jnp.sum(p * v, axis=0, keepdims=True)
            o_ref[slot, h:h + 1, :] = acc / den

    @pl.when(t == 0)
    def _():
        start(0, 0)

    wait(0)
    start(2 * t + 1, 1)
    attend(2 * t, 0)
    wait(1)
    start(jnp.minimum(2 * t + 2, 2 * nt - 1), 0)
    attend(2 * t + 1, 1)

    @pl.when(t == nt - 1)
    def _():
        wait(0)


def _moba_attend(sel_pages, sel_blocks, ck, cv, q, knew, vnew, tlast_col, bias0, n_blocks):
    nseq = q.shape[0]
    assert nseq % 2 == 0
    pool, rows, _ = ck.shape
    page = rows // A_KV_HEADS
    nsp = MOBA_TOPK * (MOBA_BLOCK // page)
    hbm = pl.BlockSpec(memory_space=pl.ANY)
    pair = pl.BlockSpec((2, A_HEADS, A_HEAD_DIM), lambda t, pg, blk: (t, 0, 0))
    full = lambda a: pl.BlockSpec(a.shape, lambda t, pg, blk: (0,) * a.ndim)
    grid_spec = pltpu.PrefetchScalarGridSpec(
        num_scalar_prefetch=2,
        grid=(nseq // 2,),
        in_specs=[hbm, hbm, pair, pair, pair, full(tlast_col), full(bias0)],
        out_specs=pair,
        scratch_shapes=[pltpu.VMEM((2, A_HEADS * nsp, rows, A_HEAD_DIM), F32),
                        pltpu.VMEM((2, A_HEADS * nsp, rows, A_HEAD_DIM), F32),
                        pltpu.SemaphoreType.DMA((2, 2))],
    )
    return pl.pallas_call(
        functools.partial(_moba_attend_kernel, n_sel_pages=nsp, n_blocks=n_blocks, page=page),
        grid_spec=grid_spec,
        out_shape=jax.ShapeDtypeStruct((nseq, A_HEADS, A_HEAD_DIM), F32),
        compiler_params=_cparams(("arbitrary",)),
        name="moba_attend",
    )(sel_pages, sel_blocks, ck, cv, q, knew, vnew, tlast_col, bias0)


def _mla_decode_kernel(pt_ref, lat_hbm, kr_hbm, ks_hbm, qabs_t_ref, qabs_ref, qpe_ref, latn_ref, kpen_ref, kscn_ref,
                       o_ref, lat_buf, kr_buf, ks_buf, sem, m_ref, l_ref, acc_ref, *, pages_per_step, groups, n_chunks):
    pps = pages_per_step
    b = pl.program_id(0)
    nb = pl.num_programs(0)
    scale = B_QK_DIM ** -0.5
    ppg = pps // groups

    def chunk_copies(seq, chunk, slot):
        out = []
        for p in range(pps):
            pid = pt_ref[seq, chunk * pps + p]
            out.append(pltpu.make_async_copy(lat_hbm.at[pid], lat_buf.at[slot, p], sem.at[slot, 0]))
            out.append(pltpu.make_async_copy(kr_hbm.at[pid], kr_buf.at[slot, p], sem.at[slot, 1]))
            out.append(pltpu.make_async_copy(ks_hbm.at[pid], ks_buf.at[slot, p], sem.at[slot, 2]))
        return out

    def start(seq, chunk, slot):
        for cp in chunk_copies(seq, chunk, slot):
            cp.start()

    def wait(slot):
        for cp in chunk_copies(0, 0, slot):
            cp.wait()

    @pl.when(b == 0)
    def _():
        start(0, 0, 0)

    m_ref[...] = jnp.full(m_ref.shape, NEG, F32)
    l_ref[...] = jnp.zeros(l_ref.shape, F32)
    acc_ref[...] = jnp.zeros(acc_ref.shape, F32)
    qabs_t = qabs_t_ref[...]
    qpe = qpe_ref[...]

    def pair(k, carry):
        wait(0)
        start(b, 2 * k + 1, 1)
        attend(0)
        wait(1)
        last = k == n_chunks // 2 - 1
        start(jnp.where(last, jnp.minimum(b + 1, nb - 1), b), jnp.where(last, 0, 2 * k + 2), 0)
        attend(1)
        return carry

    def attend(slot):
        for g in range(groups):
            attend_group(slot, g)

    def attend_group(slot, g):
        s_parts, lat_pairs = [], []
        for n in range(ppg // 2):
            i0 = g * ppg + 2 * n
            latb = jnp.concatenate([lat_buf[slot, i0].astype(BF16), lat_buf[slot, i0 + 1].astype(BF16)], axis=0)
            kpe_t = jnp.concatenate([kr_buf[slot, i0], kr_buf[slot, i0 + 1]], axis=1).astype(BF16)
            ksc_t = jnp.concatenate([ks_buf[slot, i0], ks_buf[slot, i0 + 1]], axis=1)
            s = _dot(latb, qabs_t).T[:B_HEADS] + _dot(qpe, kpe_t)
            s_parts.append(s * (ksc_t * scale))
            lat_pairs.append(latb)
        s = jnp.concatenate(s_parts, axis=1)
        m_old = m_ref[g]
        m_new = jnp.maximum(m_old, jnp.max(s, axis=-1, keepdims=True))
        alpha = jnp.exp(m_old - m_new)
        p = jnp.exp(s - m_new)
        l_ref[g] = alpha * l_ref[g] + jnp.sum(p, axis=-1, keepdims=True)
        pb = p.astype(BF16)
        acc = alpha * acc_ref[g]
        w = lat_pairs[0].shape[0]
        for n in range(ppg // 2):
            acc = acc + _dot(pb[:, n * w:(n + 1) * w], lat_pairs[n])
        acc_ref[g] = acc
        m_ref[g] = m_new

    lax.fori_loop(0, n_chunks // 2, pair, 0)

    @pl.when(b == nb - 1)
    def _():
        wait(0)

    latn = latn_ref[...]
    s_new = (jnp.sum(qabs_ref[...].astype(F32) * latn.astype(BF16).astype(F32), axis=-1, keepdims=True)
             + jnp.sum(qpe.astype(F32) * kpen_ref[...].astype(BF16).astype(F32), axis=-1, keepdims=True))
    s_new = s_new * (kscn_ref[...] * scale)
    m_f = s_new
    for g in range(groups):
        m_f = jnp.maximum(m_f, m_ref[g])
    p_new = jnp.exp(s_new - m_f)
    den = p_new
    num = p_new * latn
    for g in range(groups):
        a = jnp.exp(m_ref[g] - m_f)
        den = den + a * l_ref[g]
        num = num + a * acc_ref[g]
    o_ref[...] = num / den


def _mla_decode(page_table, c_lat, c_kr_t, c_ks_t, qabs_t, qabs, qpe, latn, kpen, kscn, pages_per_step=32, groups=1):
    nseq, n_pages = page_table.shape
    page = c_lat.shape[1]
    pps = min(pages_per_step, n_pages // 2)
    n_chunks = n_pages // pps
    assert n_pages % pps == 0 and pps % (2 * groups) == 0 and n_chunks % 2 == 0

    hbm = pl.BlockSpec(memory_space=pl.ANY)
    seq3 = lambda r, w: pl.BlockSpec((None, r, w), lambda b, pt: (b, 0, 0))
    grid_spec = pltpu.PrefetchScalarGridSpec(
        num_scalar_prefetch=1,
        grid=(nseq,),
        in_specs=[hbm, hbm, hbm, seq3(KV_LORA, LANES), seq3(B_HEADS, KV_LORA), seq3(B_HEADS, ROPE_DIM),
                  seq3(1, KV_LORA), seq3(1, ROPE_DIM), seq3(B_HEADS, 1)],
        out_specs=seq3(B_HEADS, KV_LORA),
        scratch_shapes=[pltpu.VMEM((2, pps, page, KV_LORA), F32), pltpu.VMEM((2, pps, ROPE_DIM, page), F32),
                        pltpu.VMEM((2, pps, B_HEADS, page), F32), pltpu.SemaphoreType.DMA((2, 3)),
                        pltpu.VMEM((groups, B_HEADS, 1), F32), pltpu.VMEM((groups, B_HEADS, 1), F32),
                        pltpu.VMEM((groups, B_HEADS, KV_LORA), F32)],
    )
    return pl.pallas_call(
        functools.partial(_mla_decode_kernel, pages_per_step=pps, groups=groups, n_chunks=n_chunks),
        grid_spec=grid_spec,
        out_shape=jax.ShapeDtypeStruct((nseq, B_HEADS, KV_LORA), F32),
        compiler_params=_cparams(("arbitrary",)),
        name="mla_decode",
    )(page_table, c_lat, c_kr_t, c_ks_t, qabs_t, qabs, qpe, latn, kpen, kscn)


def _head_nt_kernel(a_ref, w_ref, o_ref):
    o_ref[...] = _dot_nt(a_ref[...], w_ref[...])


def _absorb_q(qm_s, wuk):
    nseq = qm_s.shape[0]
    return pl.pallas_call(
        _head_nt_kernel,
        grid=(B_HEADS,),
        in_specs=[pl.BlockSpec((nseq, NOPE_DIM), lambda h: (0, 2 * h)),
                  pl.BlockSpec((KV_LORA, NOPE_DIM), lambda h: (0, h))],
        out_specs=pl.BlockSpec((None, nseq, KV_LORA), lambda h: (h, 0, 0)),
        out_shape=jax.ShapeDtypeStruct((B_HEADS, nseq, KV_LORA), F32),
        compiler_params=_cparams(("parallel",)),
        name="absorb_q",
    )(qm_s, wuk)


def _head_nn_kernel(a_ref, w_ref, o_ref):
    o_ref[...] = _dot(a_ref[...].astype(BF16), w_ref[...]).astype(o_ref.dtype)


def _expand_o(o_lat_h, wuv):
    _, nseq, _ = o_lat_h.shape
    return pl.pallas_call(
        _head_nn_kernel,
        grid=(B_HEADS,),
        in_specs=[pl.BlockSpec((None, nseq, KV_LORA), lambda h: (h, 0, 0)),
                  pl.BlockSpec((KV_LORA, V_DIM), lambda h: (0, h))],
        out_specs=pl.BlockSpec((nseq, V_DIM), lambda h: (0, h)),
        out_shape=jax.ShapeDtypeStruct((nseq, B_HEADS * V_DIM), BF16),
        compiler_params=_cparams(("parallel",)),
        name="expand_o",
    )(o_lat_h, wuv)


def _merge_kernel(oa_ref, ob_ref, ga_ref, gb_ref, wa_ref, wb_ref, o_ref):
    a = jax.nn.sigmoid(ga_ref[...]) * _dot(oa_ref[...], wa_ref[...])
    b = jax.nn.sigmoid(gb_ref[...]) * _dot(ob_ref[...], wb_ref[...])
    o_ref[...] = (a + b).astype(o_ref.dtype)


def _merge(oa, ob, proj, wa, wb, tm):
    n = oa.shape[0]
    d = wa.shape[1]
    return pl.pallas_call(
        _merge_kernel,
        grid=(n // tm,),
        in_specs=[pl.BlockSpec((tm, oa.shape[1]), lambda i: (i, 0)), pl.BlockSpec((tm, ob.shape[1]), lambda i: (i, 0)),
                  pl.BlockSpec((tm, d), lambda i: (i, COL_GA // d)), pl.BlockSpec((tm, d), lambda i: (i, COL_GB // d)),
                  pl.BlockSpec(wa.shape, lambda i: (0, 0)), pl.BlockSpec(wb.shape, lambda i: (0, 0))],
        out_specs=pl.BlockSpec((tm, d), lambda i: (i, 0)),
        out_shape=jax.ShapeDtypeStruct((n, d), BF16),
        compiler_params=_cparams(("parallel",)),
        name="merge",
    )(oa, ob, proj, proj, wa, wb)


def _outproj_router_kernel(t_ref, x_ref, wo_ref, g_ref, wr_ref, rb_ref, h_ref, hn_ref, ti_ref, tw_ref):
    h = x_ref[...] + _dot(t_ref[...], wo_ref[...])
    h_ref[...] = h
    inv = lax.rsqrt(jnp.mean(h * h, axis=-1, keepdims=True) + EPS)
    hn = h * inv * g_ref[...]
    hn_ref[...] = hn
    scores = jax.nn.sigmoid(_dot(hn.astype(BF16), wr_ref[...]))
    lane = _lane_iota(scores.shape)
    sel = jnp.where(lane < N_EXPERTS, scores + rb_ref[...], NEG)
    lane_f = lane.astype(F32)
    ti = jnp.zeros(scores.shape, F32)
    tw = jnp.zeros(scores.shape, F32)
    for r in range(TOP_K):
        mx = jnp.max(sel, axis=-1, keepdims=True)
        idx = jnp.min(jnp.where(sel == mx, lane_f, float(LANES)), axis=-1, keepdims=True)
        pick = lane_f == idx
        w = jnp.sum(jnp.where(pick, scores, 0.0), axis=-1, keepdims=True)
        ti = jnp.where(lane == r, idx, ti)
        tw = jnp.where(lane == r, w, tw)
        sel = jnp.where(pick, -3e38, sel)
    tw = tw / jnp.sum(tw, axis=-1, keepdims=True) * ROUTED_SCALE
    ti_ref[...] = ti[:, :8].astype(jnp.int32)
    tw_ref[...] = tw[:, :8]


def _outproj_router(t, x, wo, g, wr, rb, tm):
    n, d = x.shape
    row = lambda w: pl.BlockSpec((tm, w), lambda i: (i, 0))
    const = lambda a: pl.BlockSpec(a.shape, lambda i: (0,) * a.ndim)
    return pl.pallas_call(
        _outproj_router_kernel,
        grid=(n // tm,),
        in_specs=[row(d), row(d), const(wo), const(g), const(wr), const(rb)],
        out_specs=[row(d), row(d), row(8), row(8)],
        out_shape=[jax.ShapeDtypeStruct((n, d), F32), jax.ShapeDtypeStruct((n, d), F32),
                   jax.ShapeDtypeStruct((n, 8), jnp.int32), jax.ShapeDtypeStruct((n, 8), F32)],
        compiler_params=_cparams(("parallel",)),
        name="outproj_router",
    )(t, x, wo, g, wr, rb)


def _experts_kernel(be_ref, nv_ref, x_ref, wg_ref, wu_ref, wd_ref, o_ref, wg_b, wu_b, wd_b):
    b = pl.program_id(0)
    valid = b < nv_ref[0]
    new_expert = jnp.logical_or(b == 0, be_ref[b] != be_ref[jnp.maximum(b - 1, 0)])

    @pl.when(jnp.logical_and(valid, new_expert))
    def _():
        wg_b[...] = wg_ref[...].astype(BF16)
        wu_b[...] = wu_ref[...].astype(BF16)
        wd_b[...] = wd_ref[...].astype(BF16)

    @pl.when(valid)
    def _():
        x = x_ref[...].astype(BF16)
        hid = jax.nn.silu(_dot(x, wg_b[...])) * _dot(x, wu_b[...])
        o_ref[...] = _dot(hid.astype(BF16), wd_b[...])

    @pl.when(b >= nv_ref[0])
    def _():
        o_ref[...] = jnp.zeros(o_ref.shape, o_ref.dtype)


def _experts(blk_expert, n_valid, xs, wg, wu, wd, tm):
    rows, d = xs.shape
    de = wg.shape[2]
    grid_spec = pltpu.PrefetchScalarGridSpec(
        num_scalar_prefetch=2,
        grid=(rows // tm,),
        in_specs=[pl.BlockSpec((tm, d), lambda b, be, nv: (b, 0)),
                  pl.BlockSpec((None, d, de), lambda b, be, nv: (be[b], 0, 0)),
                  pl.BlockSpec((None, d, de), lambda b, be, nv: (be[b], 0, 0)),
                  pl.BlockSpec((None, de, d), lambda b, be, nv: (be[b], 0, 0))],
        out_specs=pl.BlockSpec((tm, d), lambda b, be, nv: (b, 0)),
        scratch_shapes=[pltpu.VMEM((d, de), BF16), pltpu.VMEM((d, de), BF16), pltpu.VMEM((de, d), BF16)],
    )
    return pl.pallas_call(
        _experts_kernel,
        grid_spec=grid_spec,
        out_shape=jax.ShapeDtypeStruct((rows, d), F32),
        compiler_params=_cparams(("arbitrary",)),
        name="experts",
    )(blk_expert, n_valid, xs, wg, wu, wd)


def _shared_combine_kernel(hn_ref, h_ref, y_ref, tw_ref, wg_ref, wu_ref, wd_ref, o_ref):
    x = hn_ref[...].astype(BF16)
    hid = jax.nn.silu(_dot(x, wg_ref[...])) * _dot(x, wu_ref[...])
    acc = h_ref[...] + _dot(hid.astype(BF16), wd_ref[...])
    tw = tw_ref[...]
    for k in range(TOP_K):
        acc = acc + tw[:, k:k + 1] * y_ref[k]
    o_ref[...] = acc


def _shared_combine(hn, h, y_slots, tw, wg, wu, wd, tm):
    n, d = h.shape
    row = pl.BlockSpec((tm, d), lambda i: (i, 0))
    const = lambda a: pl.BlockSpec(a.shape, lambda i: (0, 0))
    return pl.pallas_call(
        _shared_combine_kernel,
        grid=(n // tm,),
        in_specs=[row, row, pl.BlockSpec((TOP_K, tm, d), lambda i: (0, i, 0)),
                  pl.BlockSpec((tm, tw.shape[1]), lambda i: (i, 0)), const(wg), const(wu), const(wd)],
        out_specs=row,
        out_shape=jax.ShapeDtypeStruct((n, d), F32),
        compiler_params=_cparams(("parallel",)),
        name="shared_combine",
    )(hn, h, y_slots, tw, wg, wu, wd)


def _rel_bucket(dist):
    dist = jnp.maximum(dist, 0)
    max_exact = REL_BUCKETS // 2
    d = jnp.maximum(dist, 1).astype(F32)
    large = max_exact + (jnp.log(d / max_exact) / math.log(REL_MAX_DIST / max_exact)
                         * (REL_BUCKETS - max_exact)).astype(jnp.int32)
    large = jnp.minimum(large, REL_BUCKETS - 1)
    return jnp.where(dist < max_exact, dist, large)


def _rope_table(pos):
    half = ROPE_DIM // 2
    freqs = jnp.power(ROPE_THETA, -jnp.arange(half, dtype=F32) / half)
    ang = pos.astype(F32)[:, None] * freqs
    cos, sin = jnp.cos(ang), jnp.sin(ang)
    return jnp.concatenate([cos, cos, -sin, sin], axis=-1)


def _prepare(w_in, a_q_norm, a_k_norm, b_q_a_norm, b_q_up, b_kv_a_norm, b_w_uk, b_w_uv, b_q_norm, b_k_norm):
    d = w_in.shape[0]
    half = ROPE_DIM // 2
    swap = lambda t: jnp.concatenate([t[..., half:], t[..., :half]], axis=-1)
    o = np.cumsum([0, A_Q_W, A_KV_W, A_KV_W, Q_LORA, KV_LORA, ROPE_DIM, d, d])
    aq, ak, av, bq, bkv, bkr, ga, gb = [w_in[:, o[i]:o[i + 1]] for i in range(8)]
    wcat = jnp.concatenate([ga, gb, aq, ak, av, bq, bkv, bkr, swap(bkr), jnp.zeros((d, LANES), F32)], axis=1)
    assert wcat.shape[1] == W_CAT_COLS
    q_pe_w = b_q_up[:, :, NOPE_DIM:]
    wq = jnp.concatenate([b_q_up[:, :, :NOPE_DIM], q_pe_w, swap(q_pe_w)], axis=-1)
    gq_head = jnp.concatenate([b_q_norm[:NOPE_DIM] * b_k_norm[:NOPE_DIM], b_q_norm[NOPE_DIM:], swap(b_q_norm[NOPE_DIM:])])
    return {
        "wcat": wcat.astype(BF16),
        "wq": wq.reshape(Q_LORA, B_HEADS * MLA_HEAD_PAD).astype(BF16),
        "wuk": b_w_uk.reshape(KV_LORA, B_HEADS * NOPE_DIM).astype(BF16),
        "wuv": b_w_uv.reshape(KV_LORA, B_HEADS * V_DIM).astype(BF16),
        "gaq": jnp.tile(a_q_norm, A_HEADS)[None, :],
        "gak": jnp.tile(a_k_norm, A_KV_HEADS)[None, :],
        "gbq": b_q_a_norm[None, :],
        "gbkv": b_kv_a_norm[None, :],
        "gq": jnp.tile(gq_head, B_HEADS)[None, :],
        "gk": jnp.concatenate([b_k_norm[NOPE_DIM:], swap(b_k_norm[NOPE_DIM:])])[None, :],
    }


def _toeplitz(w):
    hh, two_n = w.shape
    n = two_n // 2
    flat = jnp.tile(w, (1, n))[:, :n * (two_n - 1)]
    return flat.reshape(hh, n, two_n - 1)[:, :, :n]


def _bias_tables(rel_bias):
    dist = jnp.arange(2 * MOBA_BLOCK, dtype=jnp.int32)
    tbl = rel_bias[:, _rel_bucket(dist)] - rel_bias[:, REL_BUCKETS - 1:]
    n = MOBA_BLOCK
    hh = tbl.shape[0]
    rev = lambda t: t[:, ::-1]
    fill = jnp.zeros((hh, 1), F32)
    tdiag = _toeplitz(jnp.concatenate([tbl[:, :1], jnp.full((hh, n - 1), NEG, F32), fill, rev(tbl[:, 1:n])], axis=1))
    tprev = _toeplitz(jnp.concatenate([rev(tbl[:, 1:n + 1]), fill, rev(tbl[:, n + 1:])], axis=1))
    tlast = rev(tbl[:, 1:n + 1])
    return tdiag, tprev, tlast, tbl[:, :1]


def _dispatch(topi, tm):
    n = topi.shape[0]
    a = n * TOP_K
    e_flat = topi[:, :TOP_K].reshape(a)
    oh = (e_flat[:, None] == jnp.arange(N_EXPERTS, dtype=jnp.int32)[None, :]).astype(jnp.int32)
    csum = jnp.cumsum(oh, axis=0)
    counts = csum[-1]
    rank = jnp.sum(oh * csum, axis=1) - 1
    padded = (counts + tm - 1) // tm * tm
    pad_end = jnp.cumsum(padded)
    pad_start = pad_end - padded
    dest = jnp.sum(oh * pad_start[None, :], axis=1) + rank
    nb = -(-a // tm) + N_EXPERTS
    tok = jnp.arange(a, dtype=jnp.int32) // TOP_K
    row_tok = jnp.zeros((nb * tm,), jnp.int32).at[dest].set(tok)
    blk_start = jnp.arange(nb, dtype=jnp.int32) * tm
    blk_expert = jnp.minimum(jnp.sum((pad_end[None, :] <= blk_start[:, None]).astype(jnp.int32), axis=1),
                             N_EXPERTS - 1)
    n_valid = (pad_end[-1:] // tm).astype(jnp.int32)
    return dest.reshape(n, TOP_K), row_tok, blk_expert, n_valid


def _tile(n, pref):
    t = pref
    while n % t:
        t //= 2
    return t


def kernel(x_prompt, x_sample, cache_moba_k, cache_moba_v, cache_mla_latent, cache_mla_krope, cache_mla_kscale,
           page_table, rel_bias, attn_norm, w_in, a_q_norm, a_k_norm, b_q_a_norm, b_q_up, b_kv_a_norm, b_w_uk,
           b_w_uv, b_q_norm, b_k_norm, w_branch_a, w_branch_b, w_out, ffn_norm, w_router, router_bias,
           w_e_gate, w_e_up, w_e_down, w_s_gate, w_s_up, w_s_down):
    depth = w_in.shape[0]
    assert depth == 1
    b_p, s_p, d = x_prompt.shape
    b_d, s_d, _ = x_sample.shape
    assert b_p == 1 and s_d == 1 and s_p % MOBA_BLOCK == 0
    pool, page = cache_moba_k.shape[1], cache_moba_k.shape[2]
    n_pages = page_table.shape[1]
    past = n_pages * page
    assert past % MOBA_BLOCK == 0 and MOBA_BLOCK % page == 0
    l = 0

    prm = _prepare(w_in[l], a_q_norm[l], a_k_norm[l], b_q_a_norm[l], b_q_up[l], b_kv_a_norm[l], b_w_uk[l],
                   b_w_uv[l], b_q_norm[l], b_k_norm[l])
    tdiag, tprev, tlast, bias0 = _bias_tables(rel_bias)
    g_attn = attn_norm[l][None, :]
    wa = w_branch_a[l].astype(BF16)
    wb = w_branch_b[l].astype(BF16)
    wo = w_out[l].astype(BF16)
    g_ffn = ffn_norm[l][None, :]
    wr = jnp.pad(w_router[l], ((0, 0), (0, LANES - N_EXPERTS))).astype(BF16)
    rb = jnp.pad(router_bias[l], (0, LANES - N_EXPERTS))[None, :]
    wsg, wsu, wsd = w_s_gate[l].astype(BF16), w_s_up[l].astype(BF16), w_s_down[l].astype(BF16)

    def token_front(x, pos, tm_front):
        n = x.shape[0]
        tm = _tile(n, 512)
        xn = _rmsnorm_bf16(x, g_attn, tm)
        proj = _matmul(xn, prm["wcat"], tm, W_CAT_COLS // 6)
        return proj, _front(proj, _rope_table(pos), prm, tm_front)

    def token_back(x, proj, oa, ob):
        n = x.shape[0]
        tm = _tile(n, 512)
        t = _merge(oa, ob, proj, wa, wb, tm)
        return _outproj_router(t, x, wo, g_ffn, wr, rb, _tile(n, 256))

    xp = x_prompt.reshape(s_p, d)
    proj_p, fp = token_front(xp, jnp.arange(s_p, dtype=jnp.int32), MOBA_BLOCK)
    qa, ka, va, kaug, vbf, kmean, qm, km, vm, lat, kpe, ksc = fp
    nblk = s_p // MOBA_BLOCK
    assert nblk <= LANES
    kmean_pad = jnp.pad(kmean.reshape(nblk, A_KV_HEADS, A_HEAD_DIM).transpose(1, 0, 2),
                        ((0, 0), (0, LANES - nblk), (0, 0)))
    tdiag_g = tdiag.reshape(A_KV_HEADS * A_GROUP, MOBA_BLOCK, MOBA_BLOCK)
    oa_p = _moba_prompt(qa, kaug, vbf, kmean_pad, tdiag_g, tprev)
    tq_mla = _tile(s_p, 2048)
    ob_p = _mla_prompt(qm, km, vm, tq_mla, tq_mla // 256)
    h_p, hn_p, ti_p, tw_p = token_back(xp, proj_p, oa_p, ob_p)

    xd = x_sample.reshape(b_d, d)
    pos_d = jnp.full((b_d,), past, jnp.int32)
    proj_d, fd = token_front(xd, pos_d, _tile(b_d, 256))
    qa_d, ka_d, va_d, _, _, _, qm_d, _, _, lat_d, kpe_d, ksc_d = fd
    ck = cache_moba_k[l].reshape(pool, page * A_KV_HEADS, A_HEAD_DIM)
    cv = cache_moba_v[l].reshape(pool, page * A_KV_HEADS, A_HEAD_DIM)
    rep = lambda t: jnp.repeat(t.reshape(b_d, A_KV_HEADS, A_HEAD_DIM), A_GROUP, axis=1)
    sel = _moba_gate(page_table, ck, qa_d.reshape(b_d, A_HEADS, A_HEAD_DIM))[:, :, :MOBA_TOPK]
    ppb = MOBA_BLOCK // page
    sel_pg = (sel[..., None] * ppb + jnp.arange(ppb, dtype=jnp.int32)).reshape(b_d, -1)
    sel_pages = jnp.take_along_axis(page_table, sel_pg, axis=1).reshape(-1)
    oa_d = _moba_attend(sel_pages, sel.reshape(-1), ck, cv, qa_d.reshape(b_d, A_HEADS, A_HEAD_DIM), rep(ka_d),
                        rep(va_d), tlast[:, :, None], bias0[:, :, None], past // MOBA_BLOCK)
    oa_d = oa_d.reshape(b_d, A_Q_W).astype(BF16)

    qabs_h = _absorb_q(qm_d, prm["wuk"])
    qabs = qabs_h.transpose(1, 0, 2).astype(BF16)
    qabs_t = jnp.pad(qabs.transpose(0, 2, 1), ((0, 0), (0, 0), (0, LANES - B_HEADS)))
    qpe = qm_d.reshape(b_d, B_HEADS, MLA_HEAD_PAD)[:, :, NOPE_DIM:NOPE_DIM + ROPE_DIM]
    c_kr_t = jnp.swapaxes(cache_mla_krope[l], 1, 2)
    c_ks_t = jnp.swapaxes(cache_mla_kscale[l], 1, 2)
    o_lat = _mla_decode(page_table, cache_mla_latent[l], c_kr_t, c_ks_t, qabs_t, qabs, qpe,
                        lat_d[:, None, :], kpe_d[:, None, :], ksc_d[:, :, None])
    ob_d = _expand_o(o_lat.transpose(1, 0, 2), prm["wuv"])
    h_d, hn_d, ti_d, tw_d = token_back(xd, proj_d, oa_d, ob_d)

    n_all = s_p + b_d
    hn_all = jnp.concatenate([hn_p, hn_d], axis=0)
    ti_all = jnp.concatenate([ti_p, ti_d], axis=0)
    tw_all = jnp.concatenate([tw_p, tw_d], axis=0)
    del n_all, tw_all
    tm_e = 256
    dest, row_tok, blk_expert, n_valid = _dispatch(ti_all, tm_e)
    xs = hn_all[row_tok]
    yb = _experts(blk_expert, n_valid, xs, w_e_gate[l], w_e_up[l], w_e_down[l], tm_e)
    y_p = _shared_combine(hn_p, h_p, yb[dest[:s_p].T], tw_p, wsg, wsu, wsd, _tile(s_p, 128))
    y_d = _shared_combine(hn_d, h_d, yb[dest[s_p:].T], tw_d, wsg, wsu, wsd, _tile(b_d, 128))

    y_prompt = y_p.reshape(b_p, s_p, d)
    y_sample = y_d.reshape(b_d, s_d, d)
    return (y_prompt, y_sample,
            ka.reshape(1, b_p, s_p, A_KV_HEADS, A_HEAD_DIM), va.reshape(1, b_p, s_p, A_KV_HEADS, A_HEAD_DIM),
            lat.reshape(1, b_p, s_p, KV_LORA), kpe.reshape(1, b_p, s_p, ROPE_DIM), ksc.reshape(1, b_p, s_p, B_HEADS),
            ka_d.reshape(1, b_d, s_d, A_KV_HEADS, A_HEAD_DIM), va_d.reshape(1, b_d, s_d, A_KV_HEADS, A_HEAD_DIM),
            lat_d.reshape(1, b_d, s_d, KV_LORA), kpe_d.reshape(1, b_d, s_d, ROPE_DIM),
            ksc_d.reshape(1, b_d, s_d, B_HEADS))
```
